```python
import math
import jax, jax.numpy as jnp
from jax import lax
import numpy as np

D_MODEL = 4096
BATCH = 8
SEQ = 2048
DEPTH = 1

D_MIX = D_MODEL
D_HG = D_MIX // 2
D_SB = D_MIX - D_HG
HEAD_DIM = 128
HG_HEADS = D_HG // HEAD_DIM
SB_HEADS = D_SB // HEAD_DIM
D_IN_PROJ = 4 * D_HG + 3 * D_SB
HG_CHUNK = 64
SB_BLOCK = 128
N_EXPERTS = 32
TOP_K = 4
D_FF = 3 * D_MODEL // 8
MOE_BLOCK = 128
SWIGLU_ALPHA = 1.702
SWIGLU_LIMIT = 7.0
NORM_EPS = 1e-6

kernel_name = "hybrid_hgrn2_stickbreaking_moe_layer"


def rmsnorm(t, w):
    tf = t.astype(jnp.float32)
    tf = tf * lax.rsqrt(jnp.mean(tf * tf, axis=-1, keepdims=True) + NORM_EPS)
    return (tf * w.astype(jnp.float32)).astype(t.dtype)


def hgrn2_mixer(q, f_logit, v, g, lb, norm_w):
    B_, S_, _ = q.shape
    nc = S_ // HG_CHUNK
    lb = lb.astype(jnp.float32)
    f = lb + (1.0 - lb) * jax.nn.sigmoid(f_logit.astype(jnp.float32))
    log_f = jnp.log(f)
    k = 1.0 - f

    def to_chunks(t):
        t = t.astype(jnp.float32).reshape(B_, nc, HG_CHUNK, HG_HEADS, HEAD_DIM)
        return t.transpose(1, 0, 3, 2, 4)

    causal = jnp.tril(jnp.ones((HG_CHUNK, HG_CHUNK), dtype=bool))[:, :, None]

    def step(state, inp):
        qc, kc, vc, lfc = inp
        b = jnp.cumsum(lfc, axis=2)
        o_inter = jnp.einsum('bhtk,bhkv->bhtv', qc * jnp.exp(b), state)
        rel = b[:, :, :, None, :] - b[:, :, None, :, :]
        decay = jnp.exp(jnp.where(causal, rel, -jnp.inf))
        scores = jnp.einsum('bhtk,bhtsk,bhsk->bhts', qc, decay, kc)
        o = o_inter + jnp.einsum('bhts,bhsv->bhtv', scores, vc)
        b_last = b[:, :, -1:, :]
        new_state = state * jnp.exp(b_last[:, :, 0, :, None]) + jnp.einsum(
            'bhsk,bhsv->bhkv', kc * jnp.exp(b_last - b), vc)
        return new_state, o

    state0 = jnp.zeros((B_, HG_HEADS, HEAD_DIM, HEAD_DIM), jnp.float32)
    _, o = lax.scan(step, state0, (to_chunks(q), to_chunks(k), to_chunks(v), to_chunks(log_f)))
    o = o.transpose(1, 0, 3, 2, 4).reshape(B_, S_, HG_HEADS, HEAD_DIM)
    o = o * lax.rsqrt(jnp.mean(o * o, axis=-1, keepdims=True) + NORM_EPS) * norm_w.astype(jnp.float32)
    o = o.reshape(B_, S_, D_HG) * jax.nn.silu(g.astype(jnp.float32))
    return o.astype(q.dtype)


def stick_breaking_attention(q, k, v):
    S_ = q.shape[2]
    scale = HEAD_DIM ** -0.5
    outs = []
    for i in range(S_ // SB_BLOCK):
        lo, hi = i * SB_BLOCK, (i + 1) * SB_BLOCK
        qb, kb, vb = q[:, :, lo:hi], k[:, :, :hi], v[:, :, :hi]
        z = jnp.einsum('bhtd,bhsd->bhts', qb, kb).astype(jnp.float32) * scale
        strict = jnp.arange(hi)[None, :] < (lo + jnp.arange(SB_BLOCK))[:, None]
        log_keep = jnp.where(strict, -jax.nn.softplus(z), 0.0)
        log_w = z + lax.cumsum(log_keep, axis=3, reverse=True)
        w = jnp.exp(jnp.where(strict, log_w, -jnp.inf))
        outs.append(jnp.einsum('bhts,bhsd->bhtd', w.astype(vb.dtype), vb))
    return jnp.concatenate(outs, axis=2)


def hybrid_mixer(h, w_in, lb, hg_norm_w, w_out):
    B_, S_, _ = h.shape
    proj = h @ w_in
    splits = [D_HG, 2 * D_HG, 3 * D_HG, 4 * D_HG, 4 * D_HG + D_SB, 4 * D_HG + 2 * D_SB]
    hq, hf, hv, hgate, sq, sk, sv = jnp.split(proj, splits, axis=-1)
    o_hg = hgrn2_mixer(hq, hf, hv, hgate, lb, hg_norm_w)

    def heads(t):
        return t.reshape(B_, S_, SB_HEADS, HEAD_DIM).transpose(0, 2, 1, 3)
    o_sb = stick_breaking_attention(heads(sq), heads(sk), heads(sv))
    o_sb = o_sb.transpose(0, 2, 1, 3).reshape(B_, S_, D_SB)
    return jnp.concatenate([o_hg, o_sb.astype(o_hg.dtype)], axis=-1) @ w_out


def clamped_swiglu(glu, lin):
    glu = jnp.minimum(glu, SWIGLU_LIMIT)
    lin = jnp.clip(lin, -SWIGLU_LIMIT, SWIGLU_LIMIT)
    return glu * jax.nn.sigmoid(SWIGLU_ALPHA * glu) * (lin + 1.0)


def moe_ffn(h, w_router, b_router, w_gu, b_gu, w_down, b_down):
    B_, S_, D = h.shape
    T = B_ * S_
    ht = h.reshape(T, D)
    logits = (ht @ w_router).astype(jnp.float32) + b_router.astype(jnp.float32)
    top_logit, top_idx = lax.top_k(logits, TOP_K)
    gates = jax.nn.softmax(top_logit, axis=-1)

    n_assign = T * TOP_K
    flat_e = top_idx.reshape(-1)
    flat_tok = jnp.arange(n_assign, dtype=jnp.int32) // TOP_K
    flat_g = gates.reshape(-1)
    order = jnp.argsort(flat_e)
    sorted_e = flat_e[order]
    counts = jnp.bincount(flat_e, length=N_EXPERTS)
    padded = (counts + MOE_BLOCK - 1) // MOE_BLOCK * MOE_BLOCK
    pad_end = jnp.cumsum(padded)
    pad_start = pad_end - padded
    start = jnp.cumsum(counts) - counts
    dest = pad_start[sorted_e] + jnp.arange(n_assign, dtype=jnp.int32) - start[sorted_e]
    n_rows = -(-n_assign // MOE_BLOCK) * MOE_BLOCK + N_EXPERTS * MOE_BLOCK
    n_blocks = n_rows // MOE_BLOCK
    row_tok = jnp.zeros((n_rows,), jnp.int32).at[dest].set(flat_tok[order])
    row_gate = jnp.zeros((n_rows,), jnp.float32).at[dest].set(flat_g[order])
    blk_expert = jnp.minimum(
        jnp.searchsorted(pad_end, jnp.arange(n_blocks, dtype=pad_end.dtype) * MOE_BLOCK, side='right'),
        N_EXPERTS - 1)

    def body(acc, blk):
        tok, gate, e = blk
        xb = ht[tok]
        gu = xb @ w_gu[e] + b_gu[e]
        glu, lin = jnp.split(gu, 2, axis=-1)
        y = clamped_swiglu(glu, lin) @ w_down[e] + b_down[e]
        return acc.at[tok].add(gate[:, None] * y.astype(jnp.float32)), None

    out, _ = lax.scan(body, jnp.zeros((T, D), jnp.float32),
                      (row_tok.reshape(n_blocks, MOE_BLOCK), row_gate.reshape(n_blocks, MOE_BLOCK), blk_expert))
    return out.reshape(B_, S_, D).astype(h.dtype)


def setup_inputs(seed: int = 0) -> dict:
    key = jax.random.key(seed)
    ks = jax.random.split(key, 18)
    f32 = jnp.float32
    nrm = lambda k, shape, s: jax.random.normal(k, shape, f32) * s
    return {
        "x": nrm(ks[0], (BATCH, SEQ, D_MODEL), 1.0),
        "c": nrm(ks[1], (BATCH, D_MODEL), 1.0),
        "w_mod": nrm(ks[2], (DEPTH, D_MODEL, 6 * D_MODEL), 0.5 * D_MODEL ** -0.5),
        "b_mod": nrm(ks[3], (DEPTH, 6 * D_MODEL), 0.01),
        "mix_pre_norm": 1.0 + nrm(ks[4], (DEPTH, D_MODEL), 0.1),
        "mix_post_norm": 1.0 + nrm(ks[5], (DEPTH, D_MODEL), 0.1),
        "w_in": nrm(ks[6], (DEPTH, D_MODEL, D_IN_PROJ), D_MODEL ** -0.5),
        "hg_lb": nrm(ks[7], (DEPTH + 1, D_HG), 0.1),
        "hg_norm_w": 1.0 + nrm(ks[8], (DEPTH, HEAD_DIM), 0.1),
        "w_out": nrm(ks[9], (DEPTH, D_MIX, D_MODEL), D_MIX ** -0.5),
        "ffn_pre_norm": 1.0 + nrm(ks[10], (DEPTH, D_MODEL), 0.1),
        "ffn_post_norm": 1.0 + nrm(ks[11], (DEPTH, D_MODEL), 0.1),
        "w_router": nrm(ks[12], (DEPTH, D_MODEL, N_EXPERTS), D_MODEL ** -0.5),
        "b_router": nrm(ks[13], (DEPTH, N_EXPERTS), 0.01),
        "w_gu": nrm(ks[14], (DEPTH, N_EXPERTS, D_MODEL, 2 * D_FF), D_MODEL ** -0.5),
        "b_gu": nrm(ks[15], (DEPTH, N_EXPERTS, 2 * D_FF), 0.01),
        "w_down": nrm(ks[16], (DEPTH, N_EXPERTS, D_FF, D_MODEL), D_FF ** -0.5),
        "b_down": nrm(ks[17], (DEPTH, N_EXPERTS, D_MODEL), 0.01),
    }


def reference(x, c, w_mod, b_mod, mix_pre_norm, mix_post_norm, w_in, hg_lb, hg_norm_w, w_out,
              ffn_pre_norm, ffn_post_norm, w_router, b_router, w_gu, b_gu, w_down, b_down):
    lb_table = jnp.cumsum(jax.nn.softmax(hg_lb.astype(jnp.float32), axis=0), axis=0)
    c_act = jax.nn.silu(c)
    for l in range(DEPTH):
        mod = (c_act @ w_mod[l] + b_mod[l])[:, None, :]
        sh1, sc1, g1, sh2, sc2, g2 = jnp.split(mod, 6, axis=-1)
        h = rmsnorm(x, mix_pre_norm[l]) * (1.0 + sc1) + sh1
        y = hybrid_mixer(h, w_in[l], lb_table[l], hg_norm_w[l], w_out[l])
        x = x + g1 * rmsnorm(y, mix_post_norm[l])
        h = rmsnorm(x, ffn_pre_norm[l]) * (1.0 + sc2) + sh2
        y = moe_ffn(h, w_router[l], b_router[l], w_gu[l], b_gu[l], w_down[l], b_down[l])
        x = x + g2 * rmsnorm(y, ffn_post_norm[l])
    return x
```

```python
import functools

import jax
import jax.numpy as jnp
from jax import lax
from jax.experimental import pallas as pl
from jax.experimental.pallas import tpu as pltpu

HEAD_DIM = 128
HG_CHUNK = 64
HG_SUB = 16
SB_BLOCK = 128
TOP_K = 4
SWIGLU_ALPHA = 1.702
SWIGLU_LIMIT = 7.0
NORM_EPS = 1e-6
VMEM_LIMIT_BYTES = 56 * 1024 * 1024

F32 = jnp.float32
BF16 = jnp.bfloat16


def _pick(dim, prefs):
    for p in prefs:
        if dim % p == 0:
            return p
    return dim


def _params(*sem):
    return pltpu.CompilerParams(dimension_semantics=sem, vmem_limit_bytes=VMEM_LIMIT_BYTES)


def _rms(t, w):
    return t * lax.rsqrt(jnp.mean(t * t, axis=-1, keepdims=True) + NORM_EPS) * w


def _dot(a, b):
    return jnp.dot(a, b, preferred_element_type=F32)


def _dot_nt(a, b):
    return lax.dot_general(a, b, (((1,), (1,)), ((), ())), preferred_element_type=F32)


def _mod_kernel(c_ref, w_ref, b_ref, o_ref):
    c = c_ref[...]
    a = (c * jax.nn.sigmoid(c)).astype(BF16)
    o_ref[...] = _dot(a, w_ref[...].astype(BF16)) + b_ref[...]


def _modulation(c, w_mod, b_mod):
    B, D = c.shape
    N = w_mod.shape[1]
    tn = _pick(N, (512, 256, 128))
    return pl.pallas_call(
        _mod_kernel,
        grid=(N // tn,),
        in_specs=[pl.BlockSpec((B, D), lambda j: (0, 0)),
                  pl.BlockSpec((D, tn), lambda j: (0, j)),
                  pl.BlockSpec((1, tn), lambda j: (0, j))],
        out_specs=pl.BlockSpec((B, tn), lambda j: (0, j)),
        out_shape=jax.ShapeDtypeStruct((B, N), F32),
        compiler_params=_params("arbitrary"),
        name="modulation",
    )(c, w_mod, b_mod.reshape(1, N))


def _inproj_kernel(x_ref, nw_ref, sc_ref, sh_ref, w_ref, o_ref, h_ref):
    @pl.when(pl.program_id(1) == 0)
    def _():
        h = _rms(x_ref[...], nw_ref[...]) * (1.0 + sc_ref[...]) + sh_ref[...]
        h_ref[...] = h.astype(BF16)

    o_ref[...] = _dot(h_ref[...], w_ref[...])


def _in_proj(x2, norm_w, sc, sh, w_bf, seq):
    T, D = x2.shape
    N = w_bf.shape[1]
    tm = _pick(seq, (512, 256, 128))
    tn = _pick(N, (1024, 512, 256, 128))
    return pl.pallas_call(
        _inproj_kernel,
        grid=(T // tm, N // tn),
        in_specs=[pl.BlockSpec((tm, D), lambda i, j: (i, 0)),
                  pl.BlockSpec((1, D), lambda i, j: (0, 0)),
                  pl.BlockSpec((None, 1, D), lambda i, j: ((i * tm) // seq, 0, 0)),
                  pl.BlockSpec((None, 1, D), lambda i, j: ((i * tm) // seq, 0, 0)),
                  pl.BlockSpec((D, tn), lambda i, j: (0, j))],
        out_specs=pl.BlockSpec((tm, tn), lambda i, j: (i, j)),
        out_shape=jax.ShapeDtypeStruct((T, N), F32),
        scratch_shapes=[pltpu.VMEM((tm, D), BF16)],
        compiler_params=_params("arbitrary", "arbitrary"),
        name="in_proj",
    )(x2, norm_w.reshape(1, D), sc, sh, w_bf)


def _split3(t):
    hi = t.astype(BF16)
    r1 = t - hi.astype(F32)
    mid = r1.astype(BF16)
    lo = (r1 - mid.astype(F32)).astype(BF16)
    return hi, mid, lo


def _hgrn_kernel(lb_ref, nw_ref, q_ref, f_ref, v_ref, g_ref, o_ref, st_ref, b_s, kk_s, v_s, od_s, *, n_chunks):
    C, SUB = HG_CHUNK, HG_SUB

    @pl.when(pl.program_id(2) == 0)
    def _():
        st_ref[...] = jnp.zeros_like(st_ref)

    hg = lb_ref[...]
    e = jnp.exp(hg - jnp.max(hg, axis=0, keepdims=True))
    lb = e[0:1, :] / jnp.sum(e, axis=0, keepdims=True)
    nw = nw_ref[...]
    row = lax.broadcasted_iota(jnp.int32, (C, C), 0)
    col = lax.broadcasted_iota(jnp.int32, (C, C), 1)
    tri = jnp.where(row >= col, 1.0, 0.0).astype(BF16)
    rows = lax.broadcasted_iota(jnp.int32, (C, 1), 0)
    rows_sub = lax.broadcasted_iota(jnp.int32, (SUB, 1), 0)

    def chunk(c, carry):
        r = pl.multiple_of(c * C, C)
        q = q_ref[pl.ds(r, C), :]
        v = v_ref[pl.ds(r, C), :]
        g = g_ref[pl.ds(r, C), :]
        f = lb + (1.0 - lb) * jax.nn.sigmoid(f_ref[pl.ds(r, C), :])
        kk = 1.0 - f
        hi, mid, lo = _split3(jnp.log(f))
        b = _dot(tri, hi) + _dot(tri, mid) + _dot(tri, lo)
        b_s[...] = b
        kk_s[...] = kk
        v_s[...] = v
        b_last = b_s[C - 1:C, :]
        v_bf = v.astype(BF16)
        st = st_ref[...]

        o = _dot_nt((q * jnp.exp(b)).astype(BF16), st.astype(BF16))

        p = jnp.zeros((C, C), F32)
        for i in range(1, C // SUB):
            bi = b_s[i * SUB - 1:i * SUB, :]
            in_i = (rows >= i * SUB) & (rows < (i + 1) * SUB)
            qh = jnp.where(in_i, q * jnp.exp(jnp.minimum(b - bi, 0.0)), 0.0)
            kh = jnp.where(rows < i * SUB, kk * jnp.exp(jnp.minimum(bi - b, 0.0)), 0.0)
            p = p + _dot_nt(qh.astype(BF16), kh.astype(BF16))
        o = o + _dot(p.astype(BF16), v_bf)

        for i in range(C // SUB):
            r0 = i * SUB
            qi = q[r0:r0 + SUB, :]
            bi = b[r0:r0 + SUB, :]
            acc = jnp.zeros((SUB, HEAD_DIM), F32)
            for s in range(SUB):
                bs = b_s[r0 + s:r0 + s + 1, :]
                a = qi * jnp.exp(jnp.minimum(bi - bs, 0.0)) * kk_s[r0 + s:r0 + s + 1, :]
                sc = jnp.sum(a, axis=-1, keepdims=True)
                acc = acc + jnp.where(rows_sub >= s, sc, 0.0) * v_s[r0 + s:r0 + s + 1, :]
            od_s[r0:r0 + SUB, :] = acc
        o = o + od_s[...]

        kt = (kk * jnp.exp(b_last - b)).astype(BF16)
        st_ref[...] = st * jnp.exp(b_last) + _dot(v.T.astype(BF16), kt)

        on = _rms(o, nw)
        o_ref[pl.ds(r, C), :] = (on * (g * jax.nn.sigmoid(g))).astype(BF16)
        return carry

    lax.fori_loop(0, n_chunks, chunk, 0)


def _hgrn2(proj, hg_lb, hg_norm_w, batch, seq, d_hg):
    T = proj.shape[0]
    H = d_hg // HEAD_DIM
    L = _pick(seq, (512, 256, 128, 64))
    nl = seq // L
    n_slots = hg_lb.shape[0]

    def col(off):
        return pl.BlockSpec((L, HEAD_DIM), lambda b, h, s: (b * nl + s, off * H + h))

    return pl.pallas_call(
        functools.partial(_hgrn_kernel, n_chunks=L // HG_CHUNK),
        grid=(batch, H, nl),
        in_specs=[pl.BlockSpec((n_slots, HEAD_DIM), lambda b, h, s: (0, h)),
                  pl.BlockSpec((1, HEAD_DIM), lambda b, h, s: (0, 0)),
                  col(0), col(1), col(2), col(3)],
        out_specs=pl.BlockSpec((L, HEAD_DIM), lambda b, h, s: (b * nl + s, h)),
        out_shape=jax.ShapeDtypeStruct((T, d_hg), BF16),
        scratch_shapes=[pltpu.VMEM((HEAD_DIM, HEAD_DIM), F32)] + [pltpu.VMEM((HG_CHUNK, HEAD_DIM), F32)] * 4,
        compiler_params=_params("arbitrary", "arbitrary", "arbitrary"),
        name="hgrn2",
    )(hg_lb, hg_norm_w.reshape(1, HEAD_DIM), proj, proj, proj, proj)


def _sb_kernel(q_ref, k_ref, v_ref, o_ref):
    TQ = SB_BLOCK
    i = pl.program_id(2)
    scale = HEAD_DIM ** -0.5
    q = q_ref[...].astype(BF16)
    row = lax.broadcasted_iota(jnp.int32, (TQ, TQ), 0)
    col = lax.broadcasted_iota(jnp.int32, (TQ, TQ), 1)
    suffix = jnp.where(row >= col, 1.0, 0.0).astype(BF16)

    def body(jj, carry):
        acc, tail = carry
        j = i - jj
        r = pl.multiple_of(j * TQ, TQ)
        kj = k_ref[pl.ds(r, TQ), :].astype(BF16)
        vj = v_ref[pl.ds(r, TQ), :].astype(BF16)
        z = _dot_nt(q, kj) * scale
        strict = (col + j * TQ) < (row + i * TQ)
        softplus = jnp.maximum(z, 0.0) + jnp.log(1.0 + jnp.exp(-jnp.abs(z)))
        lk = jnp.where(strict, -softplus, 0.0)
        hi = lk.astype(BF16)
        lo = (lk - hi.astype(F32)).astype(BF16)
        cs = _dot(hi, suffix) + _dot(lo, suffix) + tail
        w = jnp.exp(jnp.where(strict, z + cs, -jnp.inf))
        acc = acc + _dot(w.astype(BF16), vj)
        return acc, cs[:, 0:1]

    acc, _ = lax.fori_loop(0, i + 1, body, (jnp.zeros((TQ, HEAD_DIM), F32), jnp.zeros((TQ, 1), F32)))
    o_ref[...] = acc.astype(BF16)


def _stick_breaking(proj, batch, seq, d_hg, d_sb):
    T = proj.shape[0]
    H = d_sb // HEAD_DIM
    nq = seq // SB_BLOCK
    base = 4 * d_hg // HEAD_DIM
    return pl.pallas_call(
        _sb_kernel,
        grid=(batch, H, nq),
        in_specs=[pl.BlockSpec((SB_BLOCK, HEAD_DIM), lambda b, h, i: (b * nq + i, base + h)),
                  pl.BlockSpec((seq, HEAD_DIM), lambda b, h, i: (b, base + H + h)),
                  pl.BlockSpec((seq, HEAD_DIM), lambda b, h, i: (b, base + 2 * H + h))],
        out_specs=pl.BlockSpec((SB_BLOCK, HEAD_DIM), lambda b, h, i: (b * nq + i, h)),
        out_shape=jax.ShapeDtypeStruct((T, d_sb), BF16),
        compiler_params=_params("arbitrary", "arbitrary", "arbitrary"),
        name="stickbreak",
    )(proj, proj, proj)


def _outproj_kernel(ohg_ref, osb_ref, w_ref, x_ref, g1_ref, pw_ref, fw_ref, sc_ref, sh_ref, wr_ref,
                    x1_ref, h2_ref, lg_ref, acc_ref, *, n_hg):
    k = pl.program_id(1)

    @pl.when(k == 0)
    def _():
        acc_ref[...] = jnp.zeros_like(acc_ref)

    @pl.when(k < n_hg)
    def _():
        acc_ref[...] += _dot(ohg_ref[...], w_ref[...])

    @pl.when(k >= n_hg)
    def _():
        acc_ref[...] += _dot(osb_ref[...], w_ref[...])

    @pl.when(k == pl.num_programs(1) - 1)
    def _():
        x1 = x_ref[...] + g1_ref[...] * _rms(acc_ref[...], pw_ref[...])
        x1_ref[...] = x1
        h2 = _rms(x1, fw_ref[...]) * (1.0 + sc_ref[...]) + sh_ref[...]
        h2_ref[...] = h2
        lg_ref[...] = _dot_nt(wr_ref[...], h2.astype(BF16))


def _out_proj(o_hg, o_sb, w_out_bf, x2, g1, post_w, ffn_w, sc2, sh2, wr_t, seq):
    T, D = x2.shape
    d_hg, d_sb = o_hg.shape[1], o_sb.shape[1]
    E = wr_t.shape[0]
    tm = _pick(seq, (256, 128))
    tk = _pick(d_hg, (512, 256, 128))
    assert d_sb % tk == 0
    n_hg, n_sb = d_hg // tk, d_sb // tk
    vec = pl.BlockSpec((1, D), lambda i, k: (0, 0))
    per_b = pl.BlockSpec((None, 1, D), lambda i, k: ((i * tm) // seq, 0, 0))
    row_blk = pl.BlockSpec((tm, D), lambda i, k: (i, 0))
    return pl.pallas_call(
        functools.partial(_outproj_kernel, n_hg=n_hg),
        grid=(T // tm, n_hg + n_sb),
        in_specs=[pl.BlockSpec((tm, tk), lambda i, k: (i, jnp.minimum(k, n_hg - 1))),
                  pl.BlockSpec((tm, tk), lambda i, k: (i, jnp.maximum(k - n_hg, 0))),
                  pl.BlockSpec((tk, D), lambda i, k: (k, 0)),
                  row_blk, per_b, vec, vec, per_b, per_b,
                  pl.BlockSpec((E, D), lambda i, k: (0, 0))],
        out_specs=[row_blk, row_blk, pl.BlockSpec((E, tm), lambda i, k: (0, i))],
        out_shape=[jax.ShapeDtypeStruct((T, D), F32), jax.ShapeDtypeStruct((T, D), F32),
                   jax.ShapeDtypeStruct((E, T), F32)],
        scratch_shapes=[pltpu.VMEM((tm, D), F32)],
        compiler_params=_params("arbitrary", "arbitrary"),
        name="out_proj",
    )(o_hg, o_sb, w_out_bf, x2, g1, post_w.reshape(1, D), ffn_w.reshape(1, D), sc2, sh2, wr_t)


def _topk_kernel(lg_ref, b_ref, idx_ref, gate_ref):
    l = lg_ref[...] + b_ref[...]
    E = l.shape[0]
    eid = lax.broadcasted_iota(jnp.int32, l.shape, 0)
    tops, ids = [], []
    for _ in range(TOP_K):
        m = jnp.max(l, axis=0, keepdims=True)
        sel = jnp.min(jnp.where(l == m, eid, E), axis=0, keepdims=True)
        tops.append(m)
        ids.append(sel)
        l = jnp.where(eid == sel, -jnp.inf, l)
    ex = [jnp.exp(t - tops[0]) for t in tops]
    den = ex[0]
    for t in ex[1:]:
        den = den + t
    idx_ref[...] = jnp.concatenate(ids, axis=0)
    gate_ref[...] = jnp.concatenate([t / den for t in ex], axis=0)


def _topk(logits_t, b_router):
    E, T = logits_t.shape
    tt = _pick(T, (2048, 1024, 512, 256, 128))
    return pl.pallas_call(
        _topk_kernel,
        grid=(T // tt,),
        in_specs=[pl.BlockSpec((E, tt), lambda i: (0, i)), pl.BlockSpec((E, 1), lambda i: (0, 0))],
        out_specs=[pl.BlockSpec((TOP_K, tt), lambda i: (0, i)), pl.BlockSpec((TOP_K, tt), lambda i: (0, i))],
        out_shape=[jax.ShapeDtypeStruct((TOP_K, T), jnp.int32), jax.ShapeDtypeStruct((TOP_K, T), F32)],
        compiler_params=_params("arbitrary"),
        name="topk",
    )(logits_t, b_router.reshape(E, 1))


def _gather_kernel(tok_ref, h_ref, xs_ref, sem, *, rows):
    base = pl.program_id(0) * rows

    def row_copy(r):
        return pltpu.make_async_copy(h_ref.at[pl.ds(tok_ref[0, r], 1), :], xs_ref.at[pl.ds(base + r, 1), :], sem)

    def start(r, c):
        row_copy(r).start()
        return c

    def wait(r, c):
        row_copy(r).wait()
        return c

    lax.fori_loop(0, rows, start, 0)
    lax.fori_loop(0, rows, wait, 0)


def _moe_gather(h2, row_tok, rows):
    n_rows = row_tok.shape[0]
    D = h2.shape[1]
    nb = n_rows // rows
    return pl.pallas_call(
        functools.partial(_gather_kernel, rows=rows),
        grid=(nb,),
        in_specs=[pl.BlockSpec((None, 1, rows), lambda i: (i, 0, 0), memory_space=pltpu.SMEM),
                  pl.BlockSpec(memory_space=pl.ANY)],
        out_specs=pl.BlockSpec(memory_space=pl.ANY),
        out_shape=jax.ShapeDtypeStruct((n_rows, D), h2.dtype),
        scratch_shapes=[pltpu.SemaphoreType.DMA(())],
        compiler_params=_params("arbitrary"),
        name="moe_gather",
    )(row_tok.reshape(nb, 1, rows), h2)


def _moe_up_kernel(be_ref, nv_ref, x_ref, wg_ref, wl_ref, bg_ref, bl_ref, o_ref, xb_ref):
    i = pl.program_id(0)
    valid = i < nv_ref[0]

    @pl.when(valid & (pl.program_id(1) == 0))
    def _():
        xb_ref[...] = x_ref[...].astype(BF16)

    @pl.when(valid)
    def _():
        xb = xb_ref[...]
        glu = _dot(xb, wg_ref[...].astype(BF16)) + bg_ref[...]
        lin = _dot(xb, wl_ref[...].astype(BF16)) + bl_ref[...]
        glu = jnp.minimum(glu, SWIGLU_LIMIT)
        lin = jnp.clip(lin, -SWIGLU_LIMIT, SWIGLU_LIMIT)
        o_ref[...] = (glu * jax.nn.sigmoid(SWIGLU_ALPHA * glu) * (lin + 1.0)).astype(BF16)

    @pl.when(jnp.logical_not(valid))
    def _():
        o_ref[...] = jnp.zeros_like(o_ref)


def _moe_up(xs, w_gu, b_gu, blk_expert, n_valid, rows):
    n_rows, D = xs.shape
    E, _, two_ff = w_gu.shape
    d_ff = two_ff // 2
    tf = _pick(d_ff, (256, 128))
    nc = d_ff // tf
    nb = n_rows // rows

    def blk(i, nv):
        return jnp.minimum(i, nv[0] - 1)

    def chunk(i, c, nv):
        return jnp.where(i < nv[0], c, nc - 1)

    grid_spec = pltpu.PrefetchScalarGridSpec(
        num_scalar_prefetch=2,
        grid=(nb, nc),
        in_specs=[pl.BlockSpec((rows, D), lambda i, c, be, nv: (blk(i, nv), 0)),
                  pl.BlockSpec((None, D, tf), lambda i, c, be, nv: (be[i], 0, chunk(i, c, nv))),
                  pl.BlockSpec((None, D, tf), lambda i, c, be, nv: (be[i], 0, nc + chunk(i, c, nv))),
                  pl.BlockSpec((None, 1, tf), lambda i, c, be, nv: (be[i], 0, chunk(i, c, nv))),
                  pl.BlockSpec((None, 1, tf), lambda i, c, be, nv: (be[i], 0, nc + chunk(i, c, nv)))],
        out_specs=pl.BlockSpec((rows, tf), lambda i, c, be, nv: (i, c)),
        scratch_shapes=[pltpu.VMEM((rows, D), BF16)],
    )
    return pl.pallas_call(
        _moe_up_kernel,
        grid_spec=grid_spec,
        out_shape=jax.ShapeDtypeStruct((n_rows, d_ff), BF16),
        compiler_params=_params("arbitrary", "arbitrary"),
        name="moe_up",
    )(blk_expert, n_valid, xs, w_gu, w_gu, b_gu.reshape(E, 1, two_ff), b_gu.reshape(E, 1, two_ff))


def _moe_down_kernel(be_ref, nv_ref, a_ref, w_ref, b_ref, o_ref):
    valid = pl.program_id(0) < nv_ref[0]

    @pl.when(valid)
    def _():
        o_ref[...] = _dot(a_ref[...], w_ref[...].astype(BF16)) + b_ref[...]

    @pl.when(jnp.logical_not(valid))
    def _():
        o_ref[...] = jnp.zeros_like(o_ref)


def _moe_down(act, w_down, b_down, blk_expert, n_valid, rows):
    n_rows, d_ff = act.shape
    E, _, D = w_down.shape
    tn = _pick(D, (1024, 512, 256, 128))
    nj = D // tn
    nb = n_rows // rows

    def col(i, j, nv):
        return jnp.where(i < nv[0], j, nj - 1)

    grid_spec = pltpu.PrefetchScalarGridSpec(
        num_scalar_prefetch=2,
        grid=(nb, nj),
        in_specs=[pl.BlockSpec((rows, d_ff), lambda i, j, be, nv: (jnp.minimum(i, nv[0] - 1), 0)),
                  pl.BlockSpec((None, d_ff, tn), lambda i, j, be, nv: (be[i], 0, col(i, j, nv))),
                  pl.BlockSpec((None, 1, tn), lambda i, j, be, nv: (be[i], 0, col(i, j, nv)))],
        out_specs=pl.BlockSpec((rows, tn), lambda i, j, be, nv: (i, j)),
    )
    return pl.pallas_call(
        _moe_down_kernel,
        grid_spec=grid_spec,
        out_shape=jax.ShapeDtypeStruct((n_rows, D), F32),
        compiler_params=_params("arbitrary", "arbitrary"),
        name="moe_down",
    )(blk_expert, n_valid, act, w_down, b_down.reshape(E, 1, D))


def _combine_kernel(pos_ref, gate_ref, y_ref, x1_ref, g2_ref, nw_ref, o_ref, buf, sem, *, tm):
    def row_copy(k, r):
        return pltpu.make_async_copy(y_ref.at[pl.ds(pos_ref[k, r], 1), :], buf.at[k, pl.ds(r, 1), :], sem)

    for k in range(TOP_K):
        def start(r, c, k=k):
            row_copy(k, r).start()
            return c
        lax.fori_loop(0, tm, start, 0)
    for k in range(TOP_K):
        def wait(r, c, k=k):
            row_copy(k, r).wait()
            return c
        lax.fori_loop(0, tm, wait, 0)

    m = gate_ref[0] * buf[0]
    for k in range(1, TOP_K):
        m = m + gate_ref[k] * buf[k]
    o_ref[...] = x1_ref[...] + g2_ref[...] * _rms(m, nw_ref[...])


def _moe_combine(y, pos, gates, x1, g2, norm_w, seq):
    T, D = x1.shape
    tm = _pick(seq, (128,))
    nt = T // tm
    pos_b = pos.reshape(TOP_K, nt, tm).transpose(1, 0, 2)
    return pl.pallas_call(
        functools.partial(_combine_kernel, tm=tm),
        grid=(nt,),
        in_specs=[pl.BlockSpec((None, TOP_K, tm), lambda i: (i, 0, 0), memory_space=pltpu.SMEM),
                  pl.BlockSpec((TOP_K, tm, 1), lambda i: (0, i, 0)),
                  pl.BlockSpec(memory_space=pl.ANY),
                  pl.BlockSpec((tm, D), lambda i: (i, 0)),
                  pl.BlockSpec((None, 1, D), lambda i: ((i * tm) // seq, 0, 0)),
                  pl.BlockSpec((1, D), lambda i: (0, 0))],
        out_specs=pl.BlockSpec((tm, D), lambda i: (i, 0)),
        out_shape=jax.ShapeDtypeStruct((T, D), F32),
        scratch_shapes=[pltpu.VMEM((TOP_K, tm, D), F32), pltpu.SemaphoreType.DMA(())],
        compiler_params=_params("arbitrary"),
        name="moe_combine",
    )(pos_b, gates.reshape(TOP_K, T, 1), y, x1, g2, norm_w.reshape(1, D))


def _routing_tables(top_idx, n_experts, rows):
    T = top_idx.shape[1]
    n_assign = T * TOP_K
    flat_e = top_idx.T.reshape(-1)
    order = jnp.argsort(flat_e)
    sorted_e = flat_e[order]
    counts = jnp.bincount(flat_e, length=n_experts)
    padded = (counts + rows - 1) // rows * rows
    pad_end = jnp.cumsum(padded)
    pad_start = pad_end - padded
    start = jnp.cumsum(counts) - counts
    dest = (pad_start[sorted_e] + jnp.arange(n_assign, dtype=jnp.int32) - start[sorted_e]).astype(jnp.int32)
    n_rows = -(-n_assign // rows) * rows + n_experts * rows
    n_blocks = n_rows // rows
    row_tok = jnp.zeros((n_rows,), jnp.int32).at[dest].set((order // TOP_K).astype(jnp.int32))
    pos = jnp.zeros((n_assign,), jnp.int32).at[order].set(dest).reshape(T, TOP_K).T
    n_valid = (pad_end[-1] // rows).astype(jnp.int32)
    blk = jnp.minimum(jnp.arange(n_blocks, dtype=jnp.int32), n_valid - 1) * rows
    blk_expert = jnp.minimum(jnp.searchsorted(pad_end, blk, side='right'), n_experts - 1).astype(jnp.int32)
    return row_tok, pos, blk_expert, n_valid.reshape(1)


def kernel(x, c, w_mod, b_mod, mix_pre_norm, mix_post_norm, w_in, hg_lb, hg_norm_w, w_out, ffn_pre_norm,
           ffn_post_norm, w_router, b_router, w_gu, b_gu, w_down, b_down):
    B, S, D = x.shape
    depth = w_mod.shape[0]
    d_hg = hg_lb.shape[1]
    d_sb = w_out.shape[1] - d_hg
    E = w_router.shape[2]
    T = B * S
    assert depth == 1 and S % SB_BLOCK == 0 and d_hg % HEAD_DIM == 0 and d_sb % HEAD_DIM == 0
    moe_rows = _pick(T * TOP_K, (512, 256, 128))

    x2 = x.reshape(T, D)
    for l in range(depth):
        mod = _modulation(c, w_mod[l], b_mod[l])
        sh1, sc1, g1, sh2, sc2, g2 = [mod[:, j * D:(j + 1) * D].reshape(B, 1, D) for j in range(6)]

        proj = _in_proj(x2, mix_pre_norm[l], sc1, sh1, w_in[l].astype(BF16), S)
        o_hg = _hgrn2(proj, hg_lb, hg_norm_w[l], B, S, d_hg)
        o_sb = _stick_breaking(proj, B, S, d_hg, d_sb)
        x1, h2, logits_t = _out_proj(o_hg, o_sb, w_out[l].astype(BF16), x2, g1, mix_post_norm[l],
                                     ffn_pre_norm[l], sc2, sh2, w_router[l].T.astype(BF16), S)

        top_idx, gates = _topk(logits_t, b_router[l])
        row_tok, pos, blk_expert, n_valid = _routing_tables(top_idx, E, moe_rows)
        xs = _moe_gather(h2, row_tok, moe_rows)
        act = _moe_up(xs, w_gu[l], b_gu[l], blk_expert, n_valid, moe_rows)
        y = _moe_down(act, w_down[l], b_down[l], blk_expert, n_valid, moe_rows)
        x2 = _moe_combine(y, pos, gates, x1, g2, ffn_post_norm[l], S)
    return x2.reshape(B, S, D)
```

```python
import functools

import jax
import jax.numpy as jnp
from jax import lax
from jax.experimental import pallas as pl
from jax.experimental.pallas import tpu as pltpu

HEAD_DIM = 128
HG_CHUNK = 64
HG_SUB = 16
SB_BLOCK = 128
TOP_K = 4
SWIGLU_ALPHA = 1.702
SWIGLU_LIMIT = 7.0
NORM_EPS = 1e-6
VMEM_LIMIT_BYTES = 56 * 1024 * 1024

F32 = jnp.float32
BF16 = jnp.bfloat16
U32 = jnp.uint32


def _pick(dim, prefs):
    for p in prefs:
        if dim % p == 0:
            return p
    return dim


def _params(*sem):
    return pltpu.CompilerParams(dimension_semantics=sem, vmem_limit_bytes=VMEM_LIMIT_BYTES)


def _rms(t, w):
    return t * lax.rsqrt(jnp.mean(t * t, axis=-1, keepdims=True) + NORM_EPS) * w


def _dot(a, b):
    return jnp.dot(a, b, preferred_element_type=F32)


def _dot_nt(a, b):
    return lax.dot_general(a, b, (((1,), (1,)), ((), ())), preferred_element_type=F32)


def _pack_bf16_pair(lo, hi):
    lo_bits = lax.bitcast_convert_type(lo.astype(BF16).astype(F32), U32) >> 16
    hi_bits = lax.bitcast_convert_type(hi.astype(BF16).astype(F32), U32) & jnp.uint32(0xFFFF0000)
    return hi_bits | lo_bits


def _unpack_bf16_pair(words):
    lo = lax.bitcast_convert_type(words << 16, F32).astype(BF16)
    hi = lax.bitcast_convert_type(words & jnp.uint32(0xFFFF0000), F32).astype(BF16)
    return lo, hi


def _mod_kernel(c_ref, w_ref, b_ref, o_ref):
    c = c_ref[...]
    a = (c * jax.nn.sigmoid(c)).astype(BF16)
    o_ref[...] = _dot(a, w_ref[...].astype(BF16)) + b_ref[...]


def _modulation(c, w_mod, b_mod):
    B, D = c.shape
    N = w_mod.shape[1]
    tn = _pick(N, (512, 256, 128))
    return pl.pallas_call(
        _mod_kernel,
        grid=(N // tn,),
        in_specs=[pl.BlockSpec((B, D), lambda j: (0, 0)),
                  pl.BlockSpec((D, tn), lambda j: (0, j)),
                  pl.BlockSpec((1, tn), lambda j: (0, j))],
        out_specs=pl.BlockSpec((B, tn), lambda j: (0, j)),
        out_shape=jax.ShapeDtypeStruct((B, N), F32),
        compiler_params=_params("arbitrary"),
        name="modulation",
    )(c, w_mod, b_mod.reshape(1, N))


def _inproj_kernel(x_ref, nw_ref, sc_ref, sh_ref, w_ref, o_ref, h_ref):
    @pl.when(pl.program_id(1) == 0)
    def _():
        h = _rms(x_ref[...], nw_ref[...]) * (1.0 + sc_ref[...]) + sh_ref[...]
        h_ref[...] = h.astype(BF16)

    o_ref[...] = _dot(h_ref[...], w_ref[...])


def _in_proj(x2, norm_w, sc, sh, w_bf, seq):
    T, D = x2.shape
    N = w_bf.shape[1]
    tm = _pick(seq, (512, 256, 128))
    tn = _pick(N, (1024, 512, 256, 128))
    return pl.pallas_call(
        _inproj_kernel,
        grid=(T // tm, N // tn),
        in_specs=[pl.BlockSpec((tm, D), lambda i, j: (i, 0)),
                  pl.BlockSpec((1, D), lambda i, j: (0, 0)),
                  pl.BlockSpec((None, 1, D), lambda i, j: ((i * tm) // seq, 0, 0)),
                  pl.BlockSpec((None, 1, D), lambda i, j: ((i * tm) // seq, 0, 0)),
                  pl.BlockSpec((D, tn), lambda i, j: (0, j))],
        out_specs=pl.BlockSpec((tm, tn), lambda i, j: (i, j)),
        out_shape=jax.ShapeDtypeStruct((T, N), F32),
        scratch_shapes=[pltpu.VMEM((tm, D), BF16)],
        compiler_params=_params("arbitrary", "arbitrary"),
        name="in_proj",
    )(x2, norm_w.reshape(1, D), sc, sh, w_bf)


def _split3(t):
    hi = t.astype(BF16)
    r1 = t - hi.astype(F32)
    mid = r1.astype(BF16)
    lo = (r1 - mid.astype(F32)).astype(BF16)
    return hi, mid, lo


def _hgrn_kernel(lb_ref, nw_ref, q_ref, f_ref, v_ref, g_ref, o_ref, st_ref, b_s, kk_s, v_s, od_s, *, n_chunks, heads):
    C, SUB = HG_CHUNK, HG_SUB

    @pl.when(pl.program_id(2) == 0)
    def _():
        st_ref[...] = jnp.zeros_like(st_ref)

    hg = lb_ref[...]
    e = jnp.exp(hg - jnp.max(hg, axis=0, keepdims=True))
    lb_all = e[0:1, :] / jnp.sum(e, axis=0, keepdims=True)
    nw = nw_ref[...]
    row = lax.broadcasted_iota(jnp.int32, (C, C), 0)
    col = lax.broadcasted_iota(jnp.int32, (C, C), 1)
    tri = jnp.where(row >= col, 1.0, 0.0).astype(BF16)
    rows = lax.broadcasted_iota(jnp.int32, (C, 1), 0)
    rows_sub = lax.broadcasted_iota(jnp.int32, (SUB, 1), 0)

    def head_chunk(r, h):
        ls = slice(h * HEAD_DIM, (h + 1) * HEAD_DIM)
        lb = lb_all[:, ls]
        q = q_ref[pl.ds(r, C), ls]
        v = v_ref[pl.ds(r, C), ls]
        g = g_ref[pl.ds(r, C), ls]
        f = lb + (1.0 - lb) * jax.nn.sigmoid(f_ref[pl.ds(r, C), ls])
        kk = 1.0 - f
        hi, mid, lo = _split3(jnp.log(f))
        b = _dot(tri, hi) + _dot(tri, mid) + _dot(tri, lo)
        b_s[h] = b
        kk_s[h] = kk
        v_s[h] = v
        b_last = b_s[h, C - 1:C, :]
        v_bf = v.astype(BF16)
        st = st_ref[h]

        o = _dot_nt((q * jnp.exp(b)).astype(BF16), st.astype(BF16))

        p = jnp.zeros((C, C), F32)
        for i in range(1, C // SUB):
            bi = b_s[h, i * SUB - 1:i * SUB, :]
            in_i = (rows >= i * SUB) & (rows < (i + 1) * SUB)
            qh = jnp.where(in_i, q * jnp.exp(jnp.minimum(b - bi, 0.0)), 0.0)
            kh = jnp.where(rows < i * SUB, kk * jnp.exp(jnp.minimum(bi - b, 0.0)), 0.0)
            p = p + _dot_nt(qh.astype(BF16), kh.astype(BF16))
        o = o + _dot(p.astype(BF16), v_bf)

        for i in range(C // SUB):
            r0 = i * SUB
            qi = q[r0:r0 + SUB, :]
            bi = b[r0:r0 + SUB, :]
            acc = jnp.zeros((SUB, HEAD_DIM), F32)
            for s in range(SUB):
                bs = b_s[h, r0 + s:r0 + s + 1, :]
                a = qi * jnp.exp(jnp.minimum(bi - bs, 0.0)) * kk_s[h, r0 + s:r0 + s + 1, :]
                sc = jnp.sum(a, axis=-1, keepdims=True)
                acc = acc + jnp.where(rows_sub >= s, sc, 0.0) * v_s[h, r0 + s:r0 + s + 1, :]
            od_s[h, r0:r0 + SUB, :] = acc
        o = o + od_s[h]

        kt = (kk * jnp.exp(b_last - b)).astype(BF16)
        st_ref[h] = st * jnp.exp(b_last) + _dot(v.T.astype(BF16), kt)

        on = _rms(o, nw)
        o_ref[pl.ds(r, C), ls] = (on * (g * jax.nn.sigmoid(g))).astype(BF16)

    def chunk(c, carry):
        r = pl.multiple_of(c * C, C)
        for h in range(heads):
            head_chunk(r, h)
        return carry

    lax.fori_loop(0, n_chunks, chunk, 0)


def _hgrn2(proj, hg_lb, hg_norm_w, batch, seq, d_hg):
    T = proj.shape[0]
    H = d_hg // HEAD_DIM
    G = _pick(H, (2, 1))
    W = G * HEAD_DIM
    L = _pick(seq, (512, 256, 128, 64))
    nl = seq // L
    n_slots = hg_lb.shape[0]

    def col(off):
        return pl.BlockSpec((L, W), lambda b, h, s: (b * nl + s, off * (H // G) + h))

    return pl.pallas_call(
        functools.partial(_hgrn_kernel, n_chunks=L // HG_CHUNK, heads=G),
        grid=(batch, H // G, nl),
        in_specs=[pl.BlockSpec((n_slots, W), lambda b, h, s: (0, h)),
                  pl.BlockSpec((1, HEAD_DIM), lambda b, h, s: (0, 0)),
                  col(0), col(1), col(2), col(3)],
        out_specs=pl.BlockSpec((L, W), lambda b, h, s: (b * nl + s, h)),
        out_shape=jax.ShapeDtypeStruct((T, d_hg), BF16),
        scratch_shapes=[pltpu.VMEM((G, HEAD_DIM, HEAD_DIM), F32)] + [pltpu.VMEM((G, HG_CHUNK, HEAD_DIM), F32)] * 4,
        compiler_params=_params("arbitrary", "arbitrary", "arbitrary"),
        name="hgrn2",
    )(hg_lb, hg_norm_w.reshape(1, HEAD_DIM), proj, proj, proj, proj)


def _sb_kernel(q_ref, k_ref, v_ref, o_ref, acc_ref, tail_ref, *, heads, tq):
    TK = SB_BLOCK
    i = pl.program_id(2)
    n_kb = (i + 1) * (tq // TK)
    scale = HEAD_DIM ** -0.5
    row = lax.broadcasted_iota(jnp.int32, (tq, TK), 0)
    col = lax.broadcasted_iota(jnp.int32, (tq, TK), 1)
    r2 = lax.broadcasted_iota(jnp.int32, (2 * TK, 2 * TK), 0)
    c2 = lax.broadcasted_iota(jnp.int32, (2 * TK, 2 * TK), 1)
    r2 = jnp.where(r2 >= TK, r2 - TK, r2)
    op = jnp.where((c2 >= TK) | (r2 >= c2), 1.0, 0.0).astype(BF16)
    acc_ref[...] = jnp.zeros_like(acc_ref)
    tail_ref[...] = jnp.zeros_like(tail_ref)

    def body(jj, carry):
        j = n_kb - 1 - jj
        r = pl.multiple_of(j * TK, TK)
        strict = (col + j * TK) < (row + i * tq)
        for h in range(heads):
            ls = slice(h * HEAD_DIM, (h + 1) * HEAD_DIM)
            q = q_ref[:, ls].astype(BF16)
            kj = k_ref[pl.ds(r, TK), ls].astype(BF16)
            vj = v_ref[pl.ds(r, TK), ls].astype(BF16)
            z = _dot_nt(q, kj) * scale
            softplus = jnp.maximum(z, 0.0) + jnp.log(1.0 + jnp.exp(-jnp.abs(z)))
            lk = jnp.where(strict, -softplus, 0.0)
            hi = lk.astype(BF16)
            lo = (lk - hi.astype(F32)).astype(BF16)
            sums = _dot(jnp.concatenate([hi, lo], axis=1), op)
            cs = sums[:, :TK] + tail_ref[h]
            w = jnp.exp(jnp.where(strict, z + cs, -jnp.inf))
            acc_ref[h] += _dot(w.astype(BF16), vj)
            tail_ref[h] += sums[:, TK:]
        return carry

    lax.fori_loop(0, n_kb, body, 0)
    for h in range(heads):
        o_ref[:, h * HEAD_DIM:(h + 1) * HEAD_DIM] = acc_ref[h].astype(BF16)


def _stick_breaking(proj, batch, seq, d_hg, d_sb):
    T = proj.shape[0]
    H = d_sb // HEAD_DIM
    G = _pick(H, (4, 2, 1))
    W = G * HEAD_DIM
    assert (4 * d_hg) % W == 0
    tq = _pick(seq, (256, 128))
    nq = seq // tq
    base = 4 * d_hg // W
    return pl.pallas_call(
        functools.partial(_sb_kernel, heads=G, tq=tq),
        grid=(batch, H // G, nq),
        in_specs=[pl.BlockSpec((tq, W), lambda b, h, i: (b * nq + i, base + h)),
                  pl.BlockSpec((seq, W), lambda b, h, i: (b, base + H // G + h)),
                  pl.BlockSpec((seq, W), lambda b, h, i: (b, base + 2 * (H // G) + h))],
        out_specs=pl.BlockSpec((tq, W), lambda b, h, i: (b * nq + i, h)),
        out_shape=jax.ShapeDtypeStruct((T, d_sb), BF16),
        scratch_shapes=[pltpu.VMEM((G, tq, HEAD_DIM), F32), pltpu.VMEM((G, tq, HEAD_DIM), F32)],
        compiler_params=_params("arbitrary", "arbitrary", "arbitrary"),
        name="stickbreak",
    )(proj, proj, proj)


def _outproj_kernel(ohg_ref, osb_ref, w_ref, x_ref, g1_ref, pw_ref, fw_ref, sc_ref, sh_ref, wr_ref,
                    x1_ref, h2p_ref, lg_ref, acc_ref, *, n_hg):
    k = pl.program_id(1)

    @pl.when(k == 0)
    def _():
        acc_ref[...] = jnp.zeros_like(acc_ref)

    @pl.when(k < n_hg)
    def _():
        acc_ref[...] += _dot(ohg_ref[...], w_ref[...])

    @pl.when(k >= n_hg)
    def _():
        acc_ref[...] += _dot(osb_ref[...], w_ref[...])

    @pl.when(k == pl.num_programs(1) - 1)
    def _():
        x1 = x_ref[...] + g1_ref[...] * _rms(acc_ref[...], pw_ref[...])
        x1_ref[...] = x1
        h2 = _rms(x1, fw_ref[...]) * (1.0 + sc_ref[...]) + sh_ref[...]
        half = h2.shape[1] // 2
        h2p_ref[...] = _pack_bf16_pair(h2[:, :half], h2[:, half:])
        lg_ref[...] = _dot_nt(wr_ref[...], h2.astype(BF16))


def _out_proj(o_hg, o_sb, w_out_bf, x2, g1, post_w, ffn_w, sc2, sh2, wr_t, seq):
    T, D = x2.shape
    d_hg, d_sb = o_hg.shape[1], o_sb.shape[1]
    E = wr_t.shape[0]
    tm = _pick(seq, (256, 128))
    tk = _pick(d_hg, (512, 256, 128))
    assert d_sb % tk == 0 and (D // 2) % HEAD_DIM == 0
    n_hg, n_sb = d_hg // tk, d_sb // tk
    vec = pl.BlockSpec((1, D), lambda i, k: (0, 0))
    per_b = pl.BlockSpec((None, 1, D), lambda i, k: ((i * tm) // seq, 0, 0))
    row_blk = pl.BlockSpec((tm, D), lambda i, k: (i, 0))
    return pl.pallas_call(
        functools.partial(_outproj_kernel, n_hg=n_hg),
        grid=(T // tm, n_hg + n_sb),
        in_specs=[pl.BlockSpec((tm, tk), lambda i, k: (i, jnp.minimum(k, n_hg - 1))),
                  pl.BlockSpec((tm, tk), lambda i, k: (i, jnp.maximum(k - n_hg, 0))),
                  pl.BlockSpec((tk, D), lambda i, k: (k, 0)),
                  row_blk, per_b, vec, vec, per_b, per_b,
                  pl.BlockSpec((E, D), lambda i, k: (0, 0))],
        out_specs=[row_blk, pl.BlockSpec((tm, D // 2), lambda i, k: (i, 0)),
                   pl.BlockSpec((E, tm), lambda i, k: (0, i))],
        out_shape=[jax.ShapeDtypeStruct((T, D), F32), jax.ShapeDtypeStruct((T, D // 2), U32),
                   jax.ShapeDtypeStruct((E, T), F32)],
        scratch_shapes=[pltpu.VMEM((tm, D), F32)],
        compiler_params=_params("arbitrary", "arbitrary"),
        name="out_proj",
    )(o_hg, o_sb, w_out_bf, x2, g1, post_w.reshape(1, D), ffn_w.reshape(1, D), sc2, sh2, wr_t)


def _topk_kernel(lg_ref, b_ref, idx_ref, gate_ref):
    l = lg_ref[...] + b_ref[...]
    E = l.shape[0]
    eid = lax.broadcasted_iota(jnp.int32, l.shape, 0)
    tops, ids = [], []
    for _ in range(TOP_K):
        m = jnp.max(l, axis=0, keepdims=True)
        sel = jnp.min(jnp.where(l == m, eid, E), axis=0, keepdims=True)
        tops.append(m)
        ids.append(sel)
        l = jnp.where(eid == sel, -jnp.inf, l)
    ex = [jnp.exp(t - tops[0]) for t in tops]
    den = ex[0]
    for t in ex[1:]:
        den = den + t
    idx_ref[...] = jnp.concatenate(ids, axis=0)
    gate_ref[...] = jnp.concatenate([t / den for t in ex], axis=0)


def _topk(logits_t, b_router):
    E, T = logits_t.shape
    tt = _pick(T, (2048, 1024, 512, 256, 128))
    return pl.pallas_call(
        _topk_kernel,
        grid=(T // tt,),
        in_specs=[pl.BlockSpec((E, tt), lambda i: (0, i)), pl.BlockSpec((E, 1), lambda i: (0, 0))],
        out_specs=[pl.BlockSpec((TOP_K, tt), lambda i: (0, i)), pl.BlockSpec((TOP_K, tt), lambda i: (0, i))],
        out_shape=[jax.ShapeDtypeStruct((TOP_K, T), jnp.int32), jax.ShapeDtypeStruct((TOP_K, T), F32)],
        compiler_params=_params("arbitrary"),
        name="topk",
    )(logits_t, b_router.reshape(E, 1))


def _moe_up_kernel(be_ref, nv_ref, tok_ref, tokn_ref, h_ref, wg_ref, wl_ref, bg_ref, bl_ref, o_ref,
                   stage, xb_ref, sem, *, rows):
    i = pl.program_id(0)
    c = pl.program_id(1)
    nv = nv_ref[0]
    valid = i < nv
    half = stage.shape[2]

    def row_copy(tok, slot, r):
        return pltpu.make_async_copy(h_ref.at[pl.ds(tok, 1), :], stage.at[slot, pl.ds(r, 1), :], sem.at[slot])

    def issue(t_ref, slot):
        def start(r, carry):
            row_copy(t_ref[0, r], slot, r).start()
            return carry
        lax.fori_loop(0, rows, start, 0)

    @pl.when((i == 0) & (c == 0))
    def _():
        issue(tok_ref, 0)

    @pl.when(valid & (c == 0))
    def _():
        slot = i % 2

        def wait(r, carry):
            row_copy(0, slot, r).wait()
            return carry
        lax.fori_loop(0, rows, wait, 0)

        @pl.when(i + 1 < nv)
        def _():
            issue(tokn_ref, 1 - slot)

        lo, hi = _unpack_bf16_pair(stage[slot])
        xb_ref[:, :half] = lo
        xb_ref[:, half:] = hi

    @pl.when(valid)
    def _():
        xb = xb_ref[...]
        glu = _dot(xb, wg_ref[...].astype(BF16)) + bg_ref[...]
        lin = _dot(xb, wl_ref[...].astype(BF16)) + bl_ref[...]
        glu = jnp.minimum(glu, SWIGLU_LIMIT)
        lin = jnp.clip(lin, -SWIGLU_LIMIT, SWIGLU_LIMIT)
        o_ref[...] = (glu * jax.nn.sigmoid(SWIGLU_ALPHA * glu) * (lin + 1.0)).astype(BF16)

    @pl.when(jnp.logical_not(valid))
    def _():
        o_ref[...] = jnp.zeros_like(o_ref)


def _moe_up(h2p, row_tok, w_gu, b_gu, blk_expert, n_valid, rows):
    n_rows = row_tok.shape[0]
    half = h2p.shape[1]
    D = 2 * half
    E, _, two_ff = w_gu.shape
    d_ff = two_ff // 2
    tf = _pick(d_ff, (256, 128))
    nc = d_ff // tf
    nb = n_rows // rows
    tok3 = row_tok.reshape(nb, 1, rows)

    def chunk(i, c, nv):
        return jnp.where(i < nv[0], c, nc - 1)

    grid_spec = pltpu.PrefetchScalarGridSpec(
        num_scalar_prefetch=2,
        grid=(nb, nc),
        in_specs=[pl.BlockSpec((None, 1, rows), lambda i, c, be, nv: (i, 0, 0), memory_space=pltpu.SMEM),
                  pl.BlockSpec((None, 1, rows), lambda i, c, be, nv: (jnp.minimum(i + 1, nb - 1), 0, 0),
                               memory_space=pltpu.SMEM),
                  pl.BlockSpec(memory_space=pl.ANY),
                  pl.BlockSpec((None, D, tf), lambda i, c, be, nv: (be[i], 0, chunk(i, c, nv))),
                  pl.BlockSpec((None, D, tf), lambda i, c, be, nv: (be[i], 0, nc + chunk(i, c, nv))),
                  pl.BlockSpec((None, 1, tf), lambda i, c, be, nv: (be[i], 0, chunk(i, c, nv))),
                  pl.BlockSpec((None, 1, tf), lambda i, c, be, nv: (be[i], 0, nc + chunk(i, c, nv)))],
        out_specs=pl.BlockSpec((rows, tf), lambda i, c, be, nv: (i, c)),
        scratch_shapes=[pltpu.VMEM((2, rows, half), U32), pltpu.VMEM((rows, D), BF16),
                        pltpu.SemaphoreType.DMA((2,))],
    )
    return pl.pallas_call(
        functools.partial(_moe_up_kernel, rows=rows),
        grid_spec=grid_spec,
        out_shape=jax.ShapeDtypeStruct((n_rows, d_ff), BF16),
        compiler_params=_params("arbitrary", "arbitrary"),
        name="moe_up",
    )(blk_expert, n_valid, tok3, tok3, h2p, w_gu, w_gu, b_gu.reshape(E, 1, two_ff), b_gu.reshape(E, 1, two_ff))


def _moe_down_kernel(be_ref, nv_ref, a_ref, w_ref, b_ref, o_ref):
    valid = pl.program_id(0) < nv_ref[0]

    @pl.when(valid)
    def _():
        o_ref[...] = _dot(a_ref[...], w_ref[...].astype(BF16)) + b_ref[...]

    @pl.when(jnp.logical_not(valid))
    def _():
        o_ref[...] = jnp.zeros_like(o_ref)


def _moe_down(act, w_down, b_down, blk_expert, n_valid, rows):
    n_rows, d_ff = act.shape
    E, _, D = w_down.shape
    tn = _pick(D, (1024, 512, 256, 128))
    nj = D // tn
    nb = n_rows // rows

    def col(i, j, nv):
        return jnp.where(i < nv[0], j, nj - 1)

    grid_spec = pltpu.PrefetchScalarGridSpec(
        num_scalar_prefetch=2,
        grid=(nb, nj),
        in_specs=[pl.BlockSpec((rows, d_ff), lambda i, j, be, nv: (jnp.minimum(i, nv[0] - 1), 0)),
                  pl.BlockSpec((None, d_ff, tn), lambda i, j, be, nv: (be[i], 0, col(i, j, nv))),
                  pl.BlockSpec((None, 1, tn), lambda i, j, be, nv: (be[i], 0, col(i, j, nv)))],
        out_specs=pl.BlockSpec((rows, tn), lambda i, j, be, nv: (i, j)),
    )
    return pl.pallas_call(
        _moe_down_kernel,
        grid_spec=grid_spec,
        out_shape=jax.ShapeDtypeStruct((n_rows, D), F32),
        compiler_params=_params("arbitrary", "arbitrary"),
        name="moe_down",
    )(blk_expert, n_valid, act, w_down, b_down.reshape(E, 1, D))


def _combine_kernel(pos_ref, posn_ref, gate_ref, y_ref, x1_ref, g2_ref, nw_ref, o_ref, buf, sem, *, tm):
    i = pl.program_id(0)
    slot = i % 2

    def row_copy(src_row, s, k, r):
        return pltpu.make_async_copy(y_ref.at[pl.ds(src_row, 1), :], buf.at[s, k, pl.ds(r, 1), :], sem.at[s])

    def issue(p_ref, s):
        for k in range(TOP_K):
            def start(r, carry, k=k):
                row_copy(p_ref[k, r], s, k, r).start()
                return carry
            lax.fori_loop(0, tm, start, 0)

    @pl.when(i == 0)
    def _():
        issue(pos_ref, 0)

    for k in range(TOP_K):
        def wait(r, carry, k=k):
            row_copy(0, slot, k, r).wait()
            return carry
        lax.fori_loop(0, tm, wait, 0)

    @pl.when(i + 1 < pl.num_programs(0))
    def _():
        issue(posn_ref, 1 - slot)

    m = gate_ref[0] * buf[slot, 0]
    for k in range(1, TOP_K):
        m = m + gate_ref[k] * buf[slot, k]
    o_ref[...] = x1_ref[...] + g2_ref[...] * _rms(m, nw_ref[...])


def _moe_combine(y, pos, gates, x1, g2, norm_w, seq):
    T, D = x1.shape
    tm = _pick(seq, (128,))
    nt = T // tm
    pos_b = pos.reshape(TOP_K, nt, tm).transpose(1, 0, 2)
    return pl.pallas_call(
        functools.partial(_combine_kernel, tm=tm),
        grid=(nt,),
        in_specs=[pl.BlockSpec((None, TOP_K, tm), lambda i: (i, 0, 0), memory_space=pltpu.SMEM),
                  pl.BlockSpec((None, TOP_K, tm), lambda i: (jnp.minimum(i + 1, nt - 1), 0, 0),
                               memory_space=pltpu.SMEM),
                  pl.BlockSpec((TOP_K, tm, 1), lambda i: (0, i, 0)),
                  pl.BlockSpec(memory_space=pl.ANY),
                  pl.BlockSpec((tm, D), lambda i: (i, 0)),
                  pl.BlockSpec((None, 1, D), lambda i: ((i * tm) // seq, 0, 0)),
                  pl.BlockSpec((1, D), lambda i: (0, 0))],
        out_specs=pl.BlockSpec((tm, D), lambda i: (i, 0)),
        out_shape=jax.ShapeDtypeStruct((T, D), F32),
        scratch_shapes=[pltpu.VMEM((2, TOP_K, tm, D), F32), pltpu.SemaphoreType.DMA((2,))],
        compiler_params=_params("arbitrary"),
        name="moe_combine",
    )(pos_b, pos_b, gates.reshape(TOP_K, T, 1), y, x1, g2, norm_w.reshape(1, D))


def _routing_tables(top_idx, n_experts, rows):
    T = top_idx.shape[1]
    n_assign = T * TOP_K
    i32 = jnp.int32
    flat_e = top_idx.T.reshape(-1)
    iota = jnp.arange(n_assign, dtype=i32)
    _, order = lax.sort((flat_e, iota), num_keys=1)
    _, inv = lax.sort((order, iota), num_keys=1)
    eids = jnp.arange(n_experts, dtype=i32)
    counts = jnp.sum((flat_e[None, :] == eids[:, None]).astype(i32), axis=1)
    padded = (counts + rows - 1) // rows * rows
    pad_end = jnp.cumsum(padded)
    pad_start = pad_end - padded
    start = jnp.cumsum(counts) - counts
    n_rows = -(-n_assign // rows) * rows + n_experts * rows
    n_blocks = n_rows // rows
    pos = (pad_start[flat_e] + inv - start[flat_e]).astype(i32).reshape(T, TOP_K).T

    def expert_of(r):
        return jnp.minimum(jnp.sum((pad_end[None, :] <= r[:, None]).astype(i32), axis=1), n_experts - 1)

    r = jnp.arange(n_rows, dtype=i32)
    row_e = expert_of(r)
    off = r - pad_start[row_e]
    real = (off < counts[row_e]) & (r < pad_end[-1])
    src = jnp.clip(start[row_e] + off, 0, n_assign - 1)
    row_tok = jnp.where(real, order[src] // TOP_K, 0).astype(i32)
    n_valid = (pad_end[-1] // rows).astype(i32)
    blk = jnp.minimum(jnp.arange(n_blocks, dtype=i32), n_valid - 1) * rows
    return row_tok, pos, expert_of(blk).astype(i32), n_valid.reshape(1)


def kernel(x, c, w_mod, b_mod, mix_pre_norm, mix_post_norm, w_in, hg_lb, hg_norm_w, w_out, ffn_pre_norm,
           ffn_post_norm, w_router, b_router, w_gu, b_gu, w_down, b_down):
    B, S, D = x.shape
    depth = w_mod.shape[0]
    d_hg = hg_lb.shape[1]
    d_sb = w_out.shape[1] - d_hg
    E = w_router.shape[2]
    T = B * S
    assert depth == 1 and S % SB_BLOCK == 0 and d_hg % HEAD_DIM == 0 and d_sb % HEAD_DIM == 0
    moe_rows = _pick(T * TOP_K, (512, 256, 128))

    x2 = x.reshape(T, D)
    for l in range(depth):
        mod = _modulation(c, w_mod[l], b_mod[l])
        sh1, sc1, g1, sh2, sc2, g2 = [mod[:, j * D:(j + 1) * D].reshape(B, 1, D) for j in range(6)]

        proj = _in_proj(x2, mix_pre_norm[l], sc1, sh1, w_in[l].astype(BF16), S)
        o_hg = _hgrn2(proj, hg_lb, hg_norm_w[l], B, S, d_hg)
        o_sb = _stick_breaking(proj, B, S, d_hg, d_sb)
        x1, h2p, logits_t = _out_proj(o_hg, o_sb, w_out[l].astype(BF16), x2, g1, mix_post_norm[l],
                                      ffn_pre_norm[l], sc2, sh2, w_router[l].T.astype(BF16), S)

        top_idx, gates = _topk(logits_t, b_router[l])
        row_tok, pos, blk_expert, n_valid = _routing_tables(top_idx, E, moe_rows)
        act = _moe_up(h2p, row_tok, w_gu[l], b_gu[l], blk_expert, n_valid, moe_rows)
        y = _moe_down(act, w_down[l], b_down[l], blk_expert, n_valid, moe_rows)
        x2 = _moe_combine(y, pos, gates, x1, g2, ffn_post_norm[l], S)
    return x2.reshape(B, S, D)
```

```python
import functools

import jax
import jax.numpy as jnp
from jax import lax
from jax.experimental import pallas as pl
from jax.experimental.pallas import tpu as pltpu

HEAD_DIM = 128
HG_CHUNK = 64
HG_SUB = 16
SB_BLOCK = 128
TOP_K = 4
MOE_ROWS = 768
SWIGLU_ALPHA = 1.702
SWIGLU_LIMIT = 7.0
NORM_EPS = 1e-6
VMEM_LIMIT_BYTES = 56 * 1024 * 1024

F32 = jnp.float32
BF16 = jnp.bfloat16
U32 = jnp.uint32


def _pick(dim, prefs):
    for p in prefs:
        if dim % p == 0:
            return p
    return dim


def _params(*sem):
    return pltpu.CompilerParams(dimension_semantics=sem, vmem_limit_bytes=VMEM_LIMIT_BYTES)


def _rms(t, w):
    return t * lax.rsqrt(jnp.mean(t * t, axis=-1, keepdims=True) + NORM_EPS) * w


def _dot(a, b):
    return jnp.dot(a, b, preferred_element_type=F32)


def _dot_nt(a, b):
    return lax.dot_general(a, b, (((1,), (1,)), ((), ())), preferred_element_type=F32)


def _pack_bf16_pair(lo, hi):
    lo_bits = lax.bitcast_convert_type(lo.astype(BF16).astype(F32), U32) >> 16
    hi_bits = lax.bitcast_convert_type(hi.astype(BF16).astype(F32), U32) & jnp.uint32(0xFFFF0000)
    return hi_bits | lo_bits


def _unpack_bf16_pair(words):
    lo = lax.bitcast_convert_type(words << 16, F32).astype(BF16)
    hi = lax.bitcast_convert_type(words & jnp.uint32(0xFFFF0000), F32).astype(BF16)
    return lo, hi


def _mod_kernel(c_ref, w_ref, b_ref, o_ref):
    c = c_ref[...]
    a = (c * jax.nn.sigmoid(c)).astype(BF16)
    o_ref[...] = _dot(a, w_ref[...].astype(BF16)) + b_ref[...]


def _modulation(c, w_mod, b_mod):
    B, D = c.shape
    N = w_mod.shape[1]
    tn = _pick(N, (512, 256, 128))
    return pl.pallas_call(
        _mod_kernel,
        grid=(N // tn,),
        in_specs=[pl.BlockSpec((B, D), lambda j: (0, 0)),
                  pl.BlockSpec((D, tn), lambda j: (0, j)),
                  pl.BlockSpec((1, tn), lambda j: (0, j))],
        out_specs=pl.BlockSpec((B, tn), lambda j: (0, j)),
        out_shape=jax.ShapeDtypeStruct((B, N), F32),
        compiler_params=_params("arbitrary"),
        name="modulation",
    )(c, w_mod, b_mod.reshape(1, N))


def _inproj_kernel(x_ref, nw_ref, sc_ref, sh_ref, w_ref, o_ref, h_ref):
    @pl.when(pl.program_id(1) == 0)
    def _():
        h = _rms(x_ref[...], nw_ref[...]) * (1.0 + sc_ref[...]) + sh_ref[...]
        h_ref[...] = h.astype(BF16)

    o_ref[...] = _dot(h_ref[...], w_ref[...])


def _in_proj(x2, norm_w, sc, sh, w_bf, seq):
    T, D = x2.shape
    N = w_bf.shape[1]
    tm = _pick(seq, (512, 256, 128))
    tn = _pick(N, (1024, 512, 256, 128))
    return pl.pallas_call(
        _inproj_kernel,
        grid=(T // tm, N // tn),
        in_specs=[pl.BlockSpec((tm, D), lambda i, j: (i, 0)),
                  pl.BlockSpec((1, D), lambda i, j: (0, 0)),
                  pl.BlockSpec((None, 1, D), lambda i, j: ((i * tm) // seq, 0, 0)),
                  pl.BlockSpec((None, 1, D), lambda i, j: ((i * tm) // seq, 0, 0)),
                  pl.BlockSpec((D, tn), lambda i, j: (0, j))],
        out_specs=pl.BlockSpec((tm, tn), lambda i, j: (i, j)),
        out_shape=jax.ShapeDtypeStruct((T, N), F32),
        scratch_shapes=[pltpu.VMEM((tm, D), BF16)],
        compiler_params=_params("arbitrary", "arbitrary"),
        name="in_proj",
    )(x2, norm_w.reshape(1, D), sc, sh, w_bf)


def _split3(t):
    hi = t.astype(BF16)
    r1 = t - hi.astype(F32)
    mid = r1.astype(BF16)
    lo = (r1 - mid.astype(F32)).astype(BF16)
    return hi, mid, lo


def _hgrn_kernel(lb_ref, nw_ref, q_ref, f_ref, v_ref, g_ref, o_ref, st_ref, b_s, kk_s, v_s, od_s, *, n_chunks, heads):
    C, SUB = HG_CHUNK, HG_SUB

    @pl.when(pl.program_id(2) == 0)
    def _():
        st_ref[...] = jnp.zeros_like(st_ref)

    hg = lb_ref[...]
    e = jnp.exp(hg - jnp.max(hg, axis=0, keepdims=True))
    lb_all = e[0:1, :] / jnp.sum(e, axis=0, keepdims=True)
    nw = nw_ref[...]
    row = lax.broadcasted_iota(jnp.int32, (C, C), 0)
    col = lax.broadcasted_iota(jnp.int32, (C, C), 1)
    tri = jnp.where(row >= col, 1.0, 0.0).astype(BF16)
    rows = lax.broadcasted_iota(jnp.int32, (C, 1), 0)
    rows_sub = lax.broadcasted_iota(jnp.int32, (SUB, 1), 0)

    def lanes(h):
        return slice(h * HEAD_DIM, (h + 1) * HEAD_DIM)

    def decay(r, h):
        lb = lb_all[:, lanes(h)]
        q = q_ref[pl.ds(r, C), lanes(h)]
        v = v_ref[pl.ds(r, C), lanes(h)]
        f = lb + (1.0 - lb) * jax.nn.sigmoid(f_ref[pl.ds(r, C), lanes(h)])
        kk = 1.0 - f
        hi, mid, lo = _split3(jnp.log(f))
        b = _dot(tri, hi) + _dot(tri, mid) + _dot(tri, lo)
        b_s[h] = b
        kk_s[h] = kk
        v_s[h] = v
        return q, v, kk, b

    def matmul_terms(h, q, v, kk, b):
        b_last = b_s[h, C - 1:C, :]
        st = st_ref[h]
        o = _dot_nt((q * jnp.exp(b)).astype(BF16), st.astype(BF16))
        p = jnp.zeros((C, C), F32)
        for i in range(1, C // SUB):
            bi = b_s[h, i * SUB - 1:i * SUB, :]
            in_i = (rows >= i * SUB) & (rows < (i + 1) * SUB)
            qh = jnp.where(in_i, q * jnp.exp(jnp.minimum(b - bi, 0.0)), 0.0)
            kh = jnp.where(rows < i * SUB, kk * jnp.exp(jnp.minimum(bi - b, 0.0)), 0.0)
            p = p + _dot_nt(qh.astype(BF16), kh.astype(BF16))
        kt = (kk * jnp.exp(b_last - b)).astype(BF16)
        st_ref[h] = st * jnp.exp(b_last) + _dot(v.T.astype(BF16), kt)
        return o, p

    def pairwise(h, q, b):
        for i in range(C // SUB):
            r0 = i * SUB
            qi = q[r0:r0 + SUB, :]
            bi = b[r0:r0 + SUB, :]
            acc = jnp.zeros((SUB, HEAD_DIM), F32)
            for s in range(SUB):
                bs = b_s[h, r0 + s:r0 + s + 1, :]
                a = qi * jnp.exp(jnp.minimum(bi - bs, 0.0)) * kk_s[h, r0 + s:r0 + s + 1, :]
                sc = jnp.sum(a, axis=-1, keepdims=True)
                acc = acc + jnp.where(rows_sub >= s, sc, 0.0) * v_s[h, r0 + s:r0 + s + 1, :]
            od_s[h, r0:r0 + SUB, :] = acc

    def chunk(c, carry):
        r = pl.multiple_of(c * C, C)
        hs = range(heads)
        qvkb = [decay(r, h) for h in hs]
        op = [matmul_terms(h, *qvkb[h]) for h in hs]
        for h in hs:
            pairwise(h, qvkb[h][0], qvkb[h][3])
        for h in hs:
            o, p = op[h]
            o = o + _dot(p.astype(BF16), qvkb[h][1].astype(BF16)) + od_s[h]
            g = g_ref[pl.ds(r, C), lanes(h)]
            o_ref[pl.ds(r, C), lanes(h)] = (_rms(o, nw) * (g * jax.nn.sigmoid(g))).astype(BF16)
        return carry

    lax.fori_loop(0, n_chunks, chunk, 0)


def _hgrn2(proj, hg_lb, hg_norm_w, batch, seq, d_hg):
    T = proj.shape[0]
    H = d_hg // HEAD_DIM
    G = _pick(H, (4, 2, 1))
    W = G * HEAD_DIM
    L = _pick(seq, (512, 256, 128, 64))
    nl = seq // L
    n_slots = hg_lb.shape[0]

    def col(off):
        return pl.BlockSpec((L, W), lambda b, h, s: (b * nl + s, off * (H // G) + h))

    return pl.pallas_call(
        functools.partial(_hgrn_kernel, n_chunks=L // HG_CHUNK, heads=G),
        grid=(batch, H // G, nl),
        in_specs=[pl.BlockSpec((n_slots, W), lambda b, h, s: (0, h)),
                  pl.BlockSpec((1, HEAD_DIM), lambda b, h, s: (0, 0)),
                  col(0), col(1), col(2), col(3)],
        out_specs=pl.BlockSpec((L, W), lambda b, h, s: (b * nl + s, h)),
        out_shape=jax.ShapeDtypeStruct((T, d_hg), BF16),
        scratch_shapes=[pltpu.VMEM((G, HEAD_DIM, HEAD_DIM), F32)] + [pltpu.VMEM((G, HG_CHUNK, HEAD_DIM), F32)] * 4,
        compiler_params=_params("arbitrary", "arbitrary", "arbitrary"),
        name="hgrn2",
    )(hg_lb, hg_norm_w.reshape(1, HEAD_DIM), proj, proj, proj, proj)


def _sb_kernel(q_ref, k_ref, v_ref, o_ref, acc_ref, tail_ref, *, heads, tq):
    TK = SB_BLOCK
    i = pl.program_id(2)
    n_kb = (i + 1) * (tq // TK)
    scale = HEAD_DIM ** -0.5
    row = lax.broadcasted_iota(jnp.int32, (tq, TK), 0)
    col = lax.broadcasted_iota(jnp.int32, (tq, TK), 1)
    r2 = lax.broadcasted_iota(jnp.int32, (2 * TK, 2 * TK), 0)
    c2 = lax.broadcasted_iota(jnp.int32, (2 * TK, 2 * TK), 1)
    r2 = jnp.where(r2 >= TK, r2 - TK, r2)
    op = jnp.where((c2 >= TK) | (r2 >= c2), 1.0, 0.0).astype(BF16)
    acc_ref[...] = jnp.zeros_like(acc_ref)
    tail_ref[...] = jnp.zeros_like(tail_ref)

    def body(jj, carry):
        j = n_kb - 1 - jj
        r = pl.multiple_of(j * TK, TK)
        strict = (col + j * TK) < (row + i * tq)
        hs = range(heads)
        ls = [slice(h * HEAD_DIM, (h + 1) * HEAD_DIM) for h in hs]
        zs = [_dot_nt(q_ref[:, ls[h]].astype(BF16), k_ref[pl.ds(r, TK), ls[h]].astype(BF16)) * scale for h in hs]
        sums = []
        for z in zs:
            softplus = jnp.maximum(z, 0.0) + jnp.log(1.0 + jnp.exp(-jnp.abs(z)))
            lk = jnp.where(strict, -softplus, 0.0)
            hi = lk.astype(BF16)
            lo = (lk - hi.astype(F32)).astype(BF16)
            sums.append(_dot(jnp.concatenate([hi, lo], axis=1), op))
        for h in hs:
            cs = sums[h][:, :TK] + tail_ref[h]
            w = jnp.exp(jnp.where(strict, zs[h] + cs, -jnp.inf))
            acc_ref[h] += _dot(w.astype(BF16), v_ref[pl.ds(r, TK), ls[h]].astype(BF16))
            tail_ref[h] += sums[h][:, TK:]
        return carry

    lax.fori_loop(0, n_kb, body, 0)
    for h in range(heads):
        o_ref[:, h * HEAD_DIM:(h + 1) * HEAD_DIM] = acc_ref[h].astype(BF16)


def _stick_breaking(proj, batch, seq, d_hg, d_sb):
    T = proj.shape[0]
    H = d_sb // HEAD_DIM
    G = _pick(H, (4, 2, 1))
    W = G * HEAD_DIM
    assert (4 * d_hg) % W == 0
    tq = _pick(seq, (256, 128))
    nq = seq // tq
    base = 4 * d_hg // W
    return pl.pallas_call(
        functools.partial(_sb_kernel, heads=G, tq=tq),
        grid=(batch, H // G, nq),
        in_specs=[pl.BlockSpec((tq, W), lambda b, h, i: (b * nq + i, base + h)),
                  pl.BlockSpec((seq, W), lambda b, h, i: (b, base + H // G + h)),
                  pl.BlockSpec((seq, W), lambda b, h, i: (b, base + 2 * (H // G) + h))],
        out_specs=pl.BlockSpec((tq, W), lambda b, h, i: (b * nq + i, h)),
        out_shape=jax.ShapeDtypeStruct((T, d_sb), BF16),
        scratch_shapes=[pltpu.VMEM((G, tq, HEAD_DIM), F32), pltpu.VMEM((G, tq, HEAD_DIM), F32)],
        compiler_params=_params("arbitrary", "arbitrary", "arbitrary"),
        name="stickbreak",
    )(proj, proj, proj)


def _outproj_kernel(ohg_ref, osb_ref, w_ref, x_ref, g1_ref, pw_ref, fw_ref, sc_ref, sh_ref, wr_ref,
                    x1_ref, h2p_ref, lg_ref, acc_ref, *, n_hg):
    k = pl.program_id(1)

    @pl.when(k == 0)
    def _():
        acc_ref[...] = jnp.zeros_like(acc_ref)

    @pl.when(k < n_hg)
    def _():
        acc_ref[...] += _dot(ohg_ref[...], w_ref[...])

    @pl.when(k >= n_hg)
    def _():
        acc_ref[...] += _dot(osb_ref[...], w_ref[...])

    @pl.when(k == pl.num_programs(1) - 1)
    def _():
        x1 = x_ref[...] + g1_ref[...] * _rms(acc_ref[...], pw_ref[...])
        x1_ref[...] = x1
        h2 = _rms(x1, fw_ref[...]) * (1.0 + sc_ref[...]) + sh_ref[...]
        half = h2.shape[1] // 2
        h2p_ref[...] = _pack_bf16_pair(h2[:, :half], h2[:, half:])
        lg_ref[...] = _dot_nt(wr_ref[...], h2.astype(BF16))


def _out_proj(o_hg, o_sb, w_out_bf, x2, g1, post_w, ffn_w, sc2, sh2, wr_t, seq):
    T, D = x2.shape
    d_hg, d_sb = o_hg.shape[1], o_sb.shape[1]
    E = wr_t.shape[0]
    tm = _pick(seq, (256, 128))
    tk = _pick(d_hg, (512, 256, 128))
    assert d_sb % tk == 0 and (D // 2) % HEAD_DIM == 0
    n_hg, n_sb = d_hg // tk, d_sb // tk
    vec = pl.BlockSpec((1, D), lambda i, k: (0, 0))
    per_b = pl.BlockSpec((None, 1, D), lambda i, k: ((i * tm) // seq, 0, 0))
    row_blk = pl.BlockSpec((tm, D), lambda i, k: (i, 0))
    return pl.pallas_call(
        functools.partial(_outproj_kernel, n_hg=n_hg),
        grid=(T // tm, n_hg + n_sb),
        in_specs=[pl.BlockSpec((tm, tk), lambda i, k: (i, jnp.minimum(k, n_hg - 1))),
                  pl.BlockSpec((tm, tk), lambda i, k: (i, jnp.maximum(k - n_hg, 0))),
                  pl.BlockSpec((tk, D), lambda i, k: (k, 0)),
                  row_blk, per_b, vec, vec, per_b, per_b,
                  pl.BlockSpec((E, D), lambda i, k: (0, 0))],
        out_specs=[row_blk, pl.BlockSpec((tm, D // 2), lambda i, k: (i, 0)),
                   pl.BlockSpec((E, tm), lambda i, k: (0, i))],
        out_shape=[jax.ShapeDtypeStruct((T, D), F32), jax.ShapeDtypeStruct((T, D // 2), U32),
                   jax.ShapeDtypeStruct((E, T), F32)],
        scratch_shapes=[pltpu.VMEM((tm, D), F32)],
        compiler_params=_params("arbitrary", "arbitrary"),
        name="out_proj",
    )(o_hg, o_sb, w_out_bf, x2, g1, post_w.reshape(1, D), ffn_w.reshape(1, D), sc2, sh2, wr_t)


def _topk_kernel(lg_ref, b_ref, idx_ref, gate_ref):
    l = lg_ref[...] + b_ref[...]
    E = l.shape[0]
    eid = lax.broadcasted_iota(jnp.int32, l.shape, 0)
    tops, ids = [], []
    for _ in range(TOP_K):
        m = jnp.max(l, axis=0, keepdims=True)
        sel = jnp.min(jnp.where(l == m, eid, E), axis=0, keepdims=True)
        tops.append(m)
        ids.append(sel)
        l = jnp.where(eid == sel, -jnp.inf, l)
    ex = [jnp.exp(t - tops[0]) for t in tops]
    den = ex[0]
    for t in ex[1:]:
        den = den + t
    idx_ref[...] = jnp.concatenate(ids, axis=0)
    gate_ref[...] = jnp.concatenate([t / den for t in ex], axis=0)


def _topk(logits_t, b_router):
    E, T = logits_t.shape
    tt = _pick(T, (2048, 1024, 512, 256, 128))
    return pl.pallas_call(
        _topk_kernel,
        grid=(T // tt,),
        in_specs=[pl.BlockSpec((E, tt), lambda i: (0, i)), pl.BlockSpec((E, 1), lambda i: (0, 0))],
        out_specs=[pl.BlockSpec((TOP_K, tt), lambda i: (0, i)), pl.BlockSpec((TOP_K, tt), lambda i: (0, i))],
        out_shape=[jax.ShapeDtypeStruct((TOP_K, T), jnp.int32), jax.ShapeDtypeStruct((TOP_K, T), F32)],
        compiler_params=_params("arbitrary"),
        name="topk",
    )(logits_t, b_router.reshape(E, 1))


def _moe_up_kernel(be_ref, nv_ref, tok_ref, tokn_ref, h_ref, wg_ref, wl_ref, bg_ref, bl_ref, o_ref,
                   stage, xb_ref, sem, *, rows, per):
    i = pl.program_id(0)
    c = pl.program_id(1)
    nv = nv_ref[0]
    valid = i < nv
    half = stage.shape[1]

    def row_copy(tok, r):
        return pltpu.make_async_copy(h_ref.at[pl.ds(tok, 1), :], stage.at[pl.ds(r, 1), :], sem)

    @pl.when((i == 0) & (c == 0))
    def _():
        def start(r, carry):
            row_copy(tok_ref[0, r], r).start()
            return carry
        lax.fori_loop(0, rows, start, 0)

    @pl.when((i <= nv) & (c == 0))
    def _():
        def wait(r, carry):
            row_copy(0, r).wait()
            return carry
        lax.fori_loop(0, rows, wait, 0, unroll=8)

    @pl.when(valid & (c == 0))
    def _():
        lo, hi = _unpack_bf16_pair(stage[...])
        xb_ref[:, :half] = lo
        xb_ref[:, half:] = hi

    @pl.when(valid)
    def _():
        xb = xb_ref[...]
        glu = _dot(xb, wg_ref[...].astype(BF16)) + bg_ref[...]
        lin = _dot(xb, wl_ref[...].astype(BF16)) + bl_ref[...]
        glu = jnp.minimum(glu, SWIGLU_LIMIT)
        lin = jnp.clip(lin, -SWIGLU_LIMIT, SWIGLU_LIMIT)
        o_ref[...] = (glu * jax.nn.sigmoid(SWIGLU_ALPHA * glu) * (lin + 1.0)).astype(BF16)
        for u in range(per):
            r = c * per + u
            row_copy(tokn_ref[0, r], r).start()

    @pl.when(jnp.logical_not(valid))
    def _():
        o_ref[...] = jnp.zeros_like(o_ref)


def _moe_up(h2p, row_tok, w_gu, b_gu, blk_expert, n_valid, rows):
    n_rows = row_tok.shape[0]
    half = h2p.shape[1]
    D = 2 * half
    E, _, two_ff = w_gu.shape
    d_ff = two_ff // 2
    tf = _pick(d_ff, (256, 128))
    nc = d_ff // tf
    nb = n_rows // rows
    assert rows % nc == 0
    tok3 = row_tok.reshape(nb, 1, rows)

    def chunk(i, c, nv):
        return jnp.where(i < nv[0], c, nc - 1)

    grid_spec = pltpu.PrefetchScalarGridSpec(
        num_scalar_prefetch=2,
        grid=(nb, nc),
        in_specs=[pl.BlockSpec((None, 1, rows), lambda i, c, be, nv: (i, 0, 0), memory_space=pltpu.SMEM),
                  pl.BlockSpec((None, 1, rows), lambda i, c, be, nv: (jnp.minimum(i + 1, nb - 1), 0, 0),
                               memory_space=pltpu.SMEM),
                  pl.BlockSpec(memory_space=pl.ANY),
                  pl.BlockSpec((None, D, tf), lambda i, c, be, nv: (be[i], 0, chunk(i, c, nv))),
                  pl.BlockSpec((None, D, tf), lambda i, c, be, nv: (be[i], 0, nc + chunk(i, c, nv))),
                  pl.BlockSpec((None, 1, tf), lambda i, c, be, nv: (be[i], 0, chunk(i, c, nv))),
                  pl.BlockSpec((None, 1, tf), lambda i, c, be, nv: (be[i], 0, nc + chunk(i, c, nv)))],
        out_specs=pl.BlockSpec((rows, tf), lambda i, c, be, nv: (i, c)),
        scratch_shapes=[pltpu.VMEM((rows, half), U32), pltpu.VMEM((rows, D), BF16), pltpu.SemaphoreType.DMA(())],
    )
    return pl.pallas_call(
        functools.partial(_moe_up_kernel, rows=rows, per=rows // nc),
        grid_spec=grid_spec,
        out_shape=jax.ShapeDtypeStruct((n_rows, d_ff), BF16),
        compiler_params=_params("arbitrary", "arbitrary"),
        name="moe_up",
    )(blk_expert, n_valid, tok3, tok3, h2p, w_gu, w_gu, b_gu.reshape(E, 1, two_ff), b_gu.reshape(E, 1, two_ff))


def _moe_down_kernel(be_ref, nv_ref, a_ref, w_ref, b_ref, o_ref):
    valid = pl.program_id(0) < nv_ref[0]

    @pl.when(valid)
    def _():
        o_ref[...] = _dot(a_ref[...], w_ref[...].astype(BF16)) + b_ref[...]

    @pl.when(jnp.logical_not(valid))
    def _():
        o_ref[...] = jnp.zeros_like(o_ref)


def _moe_down(act, w_down, b_down, blk_expert, n_valid, rows):
    n_rows, d_ff = act.shape
    E, _, D = w_down.shape
    tn = _pick(D, (1024, 512, 256, 128))
    nj = D // tn
    nb = n_rows // rows

    def col(i, j, nv):
        return jnp.where(i < nv[0], j, nj - 1)

    grid_spec = pltpu.PrefetchScalarGridSpec(
        num_scalar_prefetch=2,
        grid=(nb, nj),
        in_specs=[pl.BlockSpec((rows, d_ff), lambda i, j, be, nv: (jnp.minimum(i, nv[0] - 1), 0)),
                  pl.BlockSpec((None, d_ff, tn), lambda i, j, be, nv: (be[i], 0, col(i, j, nv))),
                  pl.BlockSpec((None, 1, tn), lambda i, j, be, nv: (be[i], 0, col(i, j, nv)))],
        out_specs=pl.BlockSpec((rows, tn), lambda i, j, be, nv: (i, j)),
    )
    return pl.pallas_call(
        _moe_down_kernel,
        grid_spec=grid_spec,
        out_shape=jax.ShapeDtypeStruct((n_rows, D), F32),
        compiler_params=_params("arbitrary", "arbitrary"),
        name="moe_down",
    )(blk_expert, n_valid, act, w_down, b_down.reshape(E, 1, D))


def _combine_kernel(pos_ref, posn_ref, gate_ref, y_ref, x1_ref, g2_ref, nw_ref, o_ref, buf, sem, *, tm):
    i = pl.program_id(0)
    slot = i % 2

    def row_copy(src_row, s, k, r):
        return pltpu.make_async_copy(y_ref.at[pl.ds(src_row, 1), :], buf.at[s, k, pl.ds(r, 1), :], sem.at[s])

    def wait_block(s):
        for k in range(TOP_K):
            def wait(r, carry, k=k):
                row_copy(0, s, k, r).wait()
                return carry
            lax.fori_loop(0, tm, wait, 0, unroll=8)

    @pl.when(i == 0)
    def _():
        for k in range(TOP_K):
            def start(r, carry, k=k):
                row_copy(pos_ref[k, r], 0, k, r).start()
                return carry
            lax.fori_loop(0, tm, start, 0)

    wait_block(slot)

    grp = 16
    for g0 in range(0, tm, grp):
        for k in range(TOP_K):
            for r in range(g0, g0 + grp):
                row_copy(posn_ref[k, r], 1 - slot, k, r).start()
        rs = slice(g0, g0 + grp)
        m = gate_ref[0, rs, :] * buf[slot, 0, rs, :]
        for k in range(1, TOP_K):
            m = m + gate_ref[k, rs, :] * buf[slot, k, rs, :]
        o_ref[rs, :] = x1_ref[rs, :] + g2_ref[...] * _rms(m, nw_ref[...])

    @pl.when(i == pl.num_programs(0) - 1)
    def _():
        wait_block(1 - slot)


def _moe_combine(y, pos, gates, x1, g2, norm_w, seq):
    T, D = x1.shape
    tm = _pick(seq, (128,))
    nt = T // tm
    pos_b = pos.reshape(TOP_K, nt, tm).transpose(1, 0, 2)
    return pl.pallas_call(
        functools.partial(_combine_kernel, tm=tm),
        grid=(nt,),
        in_specs=[pl.BlockSpec((None, TOP_K, tm), lambda i: (i, 0, 0), memory_space=pltpu.SMEM),
                  pl.BlockSpec((None, TOP_K, tm), lambda i: (jnp.minimum(i + 1, nt - 1), 0, 0),
                               memory_space=pltpu.SMEM),
                  pl.BlockSpec((TOP_K, tm, 1), lambda i: (0, i, 0)),
                  pl.BlockSpec(memory_space=pl.ANY),
                  pl.BlockSpec((tm, D), lambda i: (i, 0)),
                  pl.BlockSpec((None, 1, D), lambda i: ((i * tm) // seq, 0, 0)),
                  pl.BlockSpec((1, D), lambda i: (0, 0))],
        out_specs=pl.BlockSpec((tm, D), lambda i: (i, 0)),
        out_shape=jax.ShapeDtypeStruct((T, D), F32),
        scratch_shapes=[pltpu.VMEM((2, TOP_K, tm, D), F32), pltpu.SemaphoreType.DMA((2,))],
        compiler_params=_params("arbitrary"),
        name="moe_combine",
    )(pos_b, pos_b, gates.reshape(TOP_K, T, 1), y, x1, g2, norm_w.reshape(1, D))


def _routing_tables(top_idx, n_experts, rows):
    T = top_idx.shape[1]
    n_assign = T * TOP_K
    i32 = jnp.int32
    flat_e = top_idx.T.reshape(-1)
    iota = jnp.arange(n_assign, dtype=i32)
    _, order = lax.sort((flat_e, iota), num_keys=1)
    _, inv = lax.sort((order, iota), num_keys=1)
    eids = jnp.arange(n_experts, dtype=i32)
    counts = jnp.sum((flat_e[None, :] == eids[:, None]).astype(i32), axis=1)
    padded = (counts + rows - 1) // rows * rows
    pad_end = jnp.cumsum(padded)
    pad_start = pad_end - padded
    start = jnp.cumsum(counts) - counts
    n_rows = -(-n_assign // rows) * rows + n_experts * rows
    n_blocks = n_rows // rows
    pos = (pad_start[flat_e] + inv - start[flat_e]).astype(i32).reshape(T, TOP_K).T

    def expert_of(r):
        return jnp.minimum(jnp.sum((pad_end[None, :] <= r[:, None]).astype(i32), axis=1), n_experts - 1)

    r = jnp.arange(n_rows, dtype=i32)
    row_e = expert_of(r)
    off = r - pad_start[row_e]
    real = (off < counts[row_e]) & (r < pad_end[-1])
    src = jnp.clip(start[row_e] + off, 0, n_assign - 1)
    row_tok = jnp.where(real, order[src] // TOP_K, 0).astype(i32)
    n_valid = (pad_end[-1] // rows).astype(i32)
    blk = jnp.minimum(jnp.arange(n_blocks, dtype=i32), n_valid - 1) * rows
    return row_tok, pos, expert_of(blk).astype(i32), n_valid.reshape(1)


def kernel(x, c, w_mod, b_mod, mix_pre_norm, mix_post_norm, w_in, hg_lb, hg_norm_w, w_out, ffn_pre_norm,
           ffn_post_norm, w_router, b_router, w_gu, b_gu, w_down, b_down):
    B, S, D = x.shape
    depth = w_mod.shape[0]
    d_hg = hg_lb.shape[1]
    d_sb = w_out.shape[1] - d_hg
    E = w_router.shape[2]
    T = B * S
    assert depth == 1 and S % SB_BLOCK == 0 and d_hg % HEAD_DIM == 0 and d_sb % HEAD_DIM == 0
    moe_rows = MOE_ROWS

    x2 = x.reshape(T, D)
    for l in range(depth):
        mod = _modulation(c, w_mod[l], b_mod[l])
        sh1, sc1, g1, sh2, sc2, g2 = [mod[:, j * D:(j + 1) * D].reshape(B, 1, D) for j in range(6)]

        proj = _in_proj(x2, mix_pre_norm[l], sc1, sh1, w_in[l].astype(BF16), S)
        o_hg = _hgrn2(proj, hg_lb, hg_norm_w[l], B, S, d_hg)
        o_sb = _stick_breaking(proj, B, S, d_hg, d_sb)
        x1, h2p, logits_t = _out_proj(o_hg, o_sb, w_out[l].astype(BF16), x2, g1, mix_post_norm[l],
                                      ffn_pre_norm[l], sc2, sh2, w_router[l].T.astype(BF16), S)

        top_idx, gates = _topk(logits_t, b_router[l])
        row_tok, pos, blk_expert, n_valid = _routing_tables(top_idx, E, moe_rows)
        act = _moe_up(h2p, row_tok, w_gu[l], b_gu[l], blk_expert, n_valid, moe_rows)
        y = _moe_down(act, w_down[l], b_down[l], blk_expert, n_valid, moe_rows)
        x2 = _moe_combine(y, pos, gates, x1, g2, ffn_post_norm[l], S)
    return x2.reshape(B, S, D)
```

```python
import functools
import math

import jax
import jax.numpy as jnp
from jax import lax
from jax.experimental import pallas as pl
from jax.experimental.pallas import tpu as pltpu

HEAD_DIM = 128
HG_CHUNK = 64
HG_SUB = 16
SB_BLOCK = 128
TOP_K = 4
MOE_ROWS = 768
SWIGLU_ALPHA = 1.702
SWIGLU_LIMIT = 7.0
NORM_EPS = 1e-6
VMEM_LIMIT_BYTES = 56 * 1024 * 1024

F32 = jnp.float32
BF16 = jnp.bfloat16
U32 = jnp.uint32


def _pick(dim, prefs):
    for p in prefs:
        if dim % p == 0:
            return p
    return dim


def _params(*sem):
    return pltpu.CompilerParams(dimension_semantics=sem, vmem_limit_bytes=VMEM_LIMIT_BYTES)


def _rms(t, w):
    return t * lax.rsqrt(jnp.mean(t * t, axis=-1, keepdims=True) + NORM_EPS) * w


def _dot(a, b):
    return jnp.dot(a, b, preferred_element_type=F32)


def _dot_nt(a, b):
    return lax.dot_general(a, b, (((1,), (1,)), ((), ())), preferred_element_type=F32)


def _pack_bf16_pair(lo, hi):
    lo_bits = lax.bitcast_convert_type(lo.astype(BF16).astype(F32), U32) >> 16
    hi_bits = lax.bitcast_convert_type(hi.astype(BF16).astype(F32), U32) & jnp.uint32(0xFFFF0000)
    return hi_bits | lo_bits


def _unpack_bf16_pair(words):
    lo = lax.bitcast_convert_type(words << 16, F32).astype(BF16)
    hi = lax.bitcast_convert_type(words & jnp.uint32(0xFFFF0000), F32).astype(BF16)
    return lo, hi


def _mod_kernel(c_ref, w_ref, b_ref, o_ref):
    c = c_ref[...]
    a = (c * jax.nn.sigmoid(c)).astype(BF16)
    o_ref[...] = _dot(a, w_ref[...].astype(BF16)) + b_ref[...]


def _modulation(c, w_mod, b_mod):
    B, D = c.shape
    N = w_mod.shape[1]
    tn = _pick(N, (512, 256, 128))
    return pl.pallas_call(
        _mod_kernel,
        grid=(N // tn,),
        in_specs=[pl.BlockSpec((B, D), lambda j: (0, 0)),
                  pl.BlockSpec((D, tn), lambda j: (0, j)),
                  pl.BlockSpec((1, tn), lambda j: (0, j))],
        out_specs=pl.BlockSpec((B, tn), lambda j: (0, j)),
        out_shape=jax.ShapeDtypeStruct((B, N), F32),
        compiler_params=_params("arbitrary"),
        name="modulation",
    )(c, w_mod, b_mod.reshape(1, N))


def _inproj_kernel(x_ref, nw_ref, sc_ref, sh_ref, w_ref, hg_ref, sb_ref, h_ref, *, n_hg):
    j = pl.program_id(1)

    @pl.when(j == 0)
    def _():
        h = _rms(x_ref[...], nw_ref[...]) * (1.0 + sc_ref[...]) + sh_ref[...]
        h_ref[...] = h.astype(BF16)

    @pl.when(j < n_hg)
    def _():
        hg_ref[...] = _dot(h_ref[...], w_ref[...])

    @pl.when(j >= n_hg)
    def _():
        sb_ref[...] = _dot(h_ref[...], w_ref[...]).astype(BF16)


def _in_proj(x2, norm_w, sc, sh, w_bf, seq, n_hg_cols):
    T, D = x2.shape
    N = w_bf.shape[1]
    tm = _pick(seq, (512, 256, 128))
    tn = _pick(math.gcd(n_hg_cols, N - n_hg_cols), (1024, 512, 256, 128))
    assert n_hg_cols % tn == 0 and (N - n_hg_cols) % tn == 0
    n_hg = n_hg_cols // tn
    return pl.pallas_call(
        functools.partial(_inproj_kernel, n_hg=n_hg),
        grid=(T // tm, N // tn),
        in_specs=[pl.BlockSpec((tm, D), lambda i, j: (i, 0)),
                  pl.BlockSpec((1, D), lambda i, j: (0, 0)),
                  pl.BlockSpec((None, 1, D), lambda i, j: ((i * tm) // seq, 0, 0)),
                  pl.BlockSpec((None, 1, D), lambda i, j: ((i * tm) // seq, 0, 0)),
                  pl.BlockSpec((D, tn), lambda i, j: (0, j))],
        out_specs=[pl.BlockSpec((tm, tn), lambda i, j: (i, jnp.minimum(j, n_hg - 1))),
                   pl.BlockSpec((tm, tn), lambda i, j: (i, jnp.maximum(j - n_hg, 0)))],
        out_shape=[jax.ShapeDtypeStruct((T, n_hg_cols), F32), jax.ShapeDtypeStruct((T, N - n_hg_cols), BF16)],
        scratch_shapes=[pltpu.VMEM((tm, D), BF16)],
        compiler_params=_params("arbitrary", "arbitrary"),
        name="in_proj",
    )(x2, norm_w.reshape(1, D), sc, sh, w_bf)


def _split3(t):
    hi = t.astype(BF16)
    r1 = t - hi.astype(F32)
    mid = r1.astype(BF16)
    lo = (r1 - mid.astype(F32)).astype(BF16)
    return hi, mid, lo


def _hgrn_kernel(lb_ref, nw_ref, q_ref, f_ref, v_ref, g_ref, o_ref, st_ref, b_s, kk_s, v_s, od_s, *, n_chunks, heads):
    C, SUB = HG_CHUNK, HG_SUB

    @pl.when(pl.program_id(2) == 0)
    def _():
        st_ref[...] = jnp.zeros_like(st_ref)

    hg = lb_ref[...]
    e = jnp.exp(hg - jnp.max(hg, axis=0, keepdims=True))
    lb_all = e[0:1, :] / jnp.sum(e, axis=0, keepdims=True)
    nw = nw_ref[...]
    row = lax.broadcasted_iota(jnp.int32, (C, C), 0)
    col = lax.broadcasted_iota(jnp.int32, (C, C), 1)
    tri = jnp.where(row >= col, 1.0, 0.0).astype(BF16)
    rows = lax.broadcasted_iota(jnp.int32, (C, 1), 0)
    rows_sub = lax.broadcasted_iota(jnp.int32, (SUB, 1), 0)

    def lanes(h):
        return slice(h * HEAD_DIM, (h + 1) * HEAD_DIM)

    def decay(r, h):
        lb = lb_all[:, lanes(h)]
        q = q_ref[pl.ds(r, C), lanes(h)]
        v = v_ref[pl.ds(r, C), lanes(h)]
        f = lb + (1.0 - lb) * jax.nn.sigmoid(f_ref[pl.ds(r, C), lanes(h)])
        kk = 1.0 - f
        hi, mid, lo = _split3(jnp.log(f))
        b = _dot(tri, hi) + _dot(tri, mid) + _dot(tri, lo)
        b_s[h] = b
        kk_s[h] = kk
        v_s[h] = v
        return q, v, kk, b

    def matmul_terms(h, q, v, kk, b):
        b_last = b_s[h, C - 1:C, :]
        st = st_ref[h]
        o = _dot_nt((q * jnp.exp(b)).astype(BF16), st.astype(BF16))
        p = jnp.zeros((C, C), F32)
        for i in range(1, C // SUB):
            bi = b_s[h, i * SUB - 1:i * SUB, :]
            in_i = (rows >= i * SUB) & (rows < (i + 1) * SUB)
            qh = jnp.where(in_i, q * jnp.exp(jnp.minimum(b - bi, 0.0)), 0.0)
            kh = jnp.where(rows < i * SUB, kk * jnp.exp(jnp.minimum(bi - b, 0.0)), 0.0)
            p = p + _dot_nt(qh.astype(BF16), kh.astype(BF16))
        kt = (kk * jnp.exp(b_last - b)).astype(BF16)
        st_ref[h] = st * jnp.exp(b_last) + _dot(v.T.astype(BF16), kt)
        return o, p

    def pairwise(h, q, b):
        for i in range(C // SUB):
            r0 = i * SUB
            qi = q[r0:r0 + SUB, :]
            bi = b[r0:r0 + SUB, :]
            acc = jnp.zeros((SUB, HEAD_DIM), F32)
            for s in range(SUB):
                bs = b_s[h, r0 + s:r0 + s + 1, :]
                a = qi * jnp.exp(jnp.minimum(bi - bs, 0.0)) * kk_s[h, r0 + s:r0 + s + 1, :]
                sc = jnp.sum(a, axis=-1, keepdims=True)
                acc = acc + jnp.where(rows_sub >= s, sc, 0.0) * v_s[h, r0 + s:r0 + s + 1, :]
            od_s[h, r0:r0 + SUB, :] = acc

    def chunk(c, carry):
        r = pl.multiple_of(c * C, C)
        hs = range(heads)
        qvkb = [decay(r, h) for h in hs]
        op = [matmul_terms(h, *qvkb[h]) for h in hs]
        for h in hs:
            pairwise(h, qvkb[h][0], qvkb[h][3])
        for h in hs:
            o, p = op[h]
            o = o + _dot(p.astype(BF16), qvkb[h][1].astype(BF16)) + od_s[h]
            g = g_ref[pl.ds(r, C), lanes(h)]
            o_ref[pl.ds(r, C), lanes(h)] = (_rms(o, nw) * (g * jax.nn.sigmoid(g))).astype(BF16)
        return carry

    lax.fori_loop(0, n_chunks, chunk, 0)


def _hgrn2(proj, hg_lb, hg_norm_w, batch, seq, d_hg):
    T = proj.shape[0]
    H = d_hg // HEAD_DIM
    G = _pick(H, (4, 2, 1))
    W = G * HEAD_DIM
    L = _pick(seq, (512, 256, 128, 64))
    nl = seq // L
    n_slots = hg_lb.shape[0]

    def col(off):
        return pl.BlockSpec((L, W), lambda b, h, s: (b * nl + s, off * (H // G) + h))

    return pl.pallas_call(
        functools.partial(_hgrn_kernel, n_chunks=L // HG_CHUNK, heads=G),
        grid=(batch, H // G, nl),
        in_specs=[pl.BlockSpec((n_slots, W), lambda b, h, s: (0, h)),
                  pl.BlockSpec((1, HEAD_DIM), lambda b, h, s: (0, 0)),
                  col(0), col(1), col(2), col(3)],
        out_specs=pl.BlockSpec((L, W), lambda b, h, s: (b * nl + s, h)),
        out_shape=jax.ShapeDtypeStruct((T, d_hg), BF16),
        scratch_shapes=[pltpu.VMEM((G, HEAD_DIM, HEAD_DIM), F32)] + [pltpu.VMEM((G, HG_CHUNK, HEAD_DIM), F32)] * 4,
        compiler_params=_params("arbitrary", "arbitrary", "arbitrary"),
        name="hgrn2",
    )(hg_lb, hg_norm_w.reshape(1, HEAD_DIM), proj, proj, proj, proj)


def _sb_kernel(q_ref, k_ref, v_ref, o_ref, acc_ref, tail_ref, *, heads, tq):
    TK = SB_BLOCK
    i = pl.program_id(2)
    n_kb = (i + 1) * (tq // TK)
    scale = HEAD_DIM ** -0.5
    row = lax.broadcasted_iota(jnp.int32, (tq, TK), 0)
    col = lax.broadcasted_iota(jnp.int32, (tq, TK), 1)
    r2 = lax.broadcasted_iota(jnp.int32, (2 * TK, 2 * TK), 0)
    c2 = lax.broadcasted_iota(jnp.int32, (2 * TK, 2 * TK), 1)
    r2 = jnp.where(r2 >= TK, r2 - TK, r2)
    op = jnp.where((c2 >= TK) | (r2 >= c2), 1.0, 0.0).astype(BF16)
    acc_ref[...] = jnp.zeros_like(acc_ref)
    tail_ref[...] = jnp.zeros_like(tail_ref)

    def body(jj, carry):
        j = n_kb - 1 - jj
        r = pl.multiple_of(j * TK, TK)
        strict = (col + j * TK) < (row + i * tq)
        hs = range(heads)
        ls = [slice(h * HEAD_DIM, (h + 1) * HEAD_DIM) for h in hs]
        zs = [_dot_nt(q_ref[:, ls[h]], k_ref[pl.ds(r, TK), ls[h]]) * scale for h in hs]
        sums = []
        for z in zs:
            softplus = jnp.maximum(z, 0.0) + jnp.log(1.0 + jnp.exp(-jnp.abs(z)))
            lk = jnp.where(strict, -softplus, 0.0)
            hi = lk.astype(BF16)
            lo = (lk - hi.astype(F32)).astype(BF16)
            sums.append(_dot(jnp.concatenate([hi, lo], axis=1), op))
        for h in hs:
            cs = sums[h][:, :TK] + tail_ref[h]
            w = jnp.exp(jnp.where(strict, zs[h] + cs, -jnp.inf))
            acc_ref[h] += _dot(w.astype(BF16), v_ref[pl.ds(r, TK), ls[h]])
            tail_ref[h] += sums[h][:, TK:]
        return carry

    lax.fori_loop(0, n_kb, body, 0)
    for h in range(heads):
        o_ref[:, h * HEAD_DIM:(h + 1) * HEAD_DIM] = acc_ref[h].astype(BF16)


def _stick_breaking(proj, batch, seq, d_sb):
    T = proj.shape[0]
    H = d_sb // HEAD_DIM
    G = _pick(H, (4, 2, 1))
    W = G * HEAD_DIM
    tq = _pick(seq, (256, 128))
    nq = seq // tq
    base = 0
    return pl.pallas_call(
        functools.partial(_sb_kernel, heads=G, tq=tq),
        grid=(batch, H // G, nq),
        in_specs=[pl.BlockSpec((tq, W), lambda b, h, i: (b * nq + i, base + h)),
                  pl.BlockSpec((seq, W), lambda b, h, i: (b, base + H // G + h)),
                  pl.BlockSpec((seq, W), lambda b, h, i: (b, base + 2 * (H // G) + h))],
        out_specs=pl.BlockSpec((tq, W), lambda b, h, i: (b * nq + i, h)),
        out_shape=jax.ShapeDtypeStruct((T, d_sb), BF16),
        scratch_shapes=[pltpu.VMEM((G, tq, HEAD_DIM), F32), pltpu.VMEM((G, tq, HEAD_DIM), F32)],
        compiler_params=_params("arbitrary", "arbitrary", "arbitrary"),
        name="stickbreak",
    )(proj, proj, proj)


def _outproj_kernel(ohg_ref, osb_ref, w_ref, x_ref, g1_ref, pw_ref, fw_ref, sc_ref, sh_ref, wr_ref,
                    x1_ref, h2p_ref, lg_ref, *, n_hg):
    k = pl.program_id(1)

    @pl.when(k == 0)
    def _():
        x1_ref[...] = jnp.zeros_like(x1_ref)

    @pl.when(k < n_hg)
    def _():
        x1_ref[...] += _dot(ohg_ref[...], w_ref[...])

    @pl.when(k >= n_hg)
    def _():
        x1_ref[...] += _dot(osb_ref[...], w_ref[...])

    @pl.when(k == pl.num_programs(1) - 1)
    def _():
        x1 = x_ref[...] + g1_ref[...] * _rms(x1_ref[...], pw_ref[...])
        x1_ref[...] = x1
        h2 = _rms(x1, fw_ref[...]) * (1.0 + sc_ref[...]) + sh_ref[...]
        half = h2.shape[1] // 2
        h2p_ref[...] = _pack_bf16_pair(h2[:, :half], h2[:, half:])
        lg_ref[...] = _dot_nt(wr_ref[...], h2.astype(BF16))


def _out_proj(o_hg, o_sb, w_out_bf, x2, g1, post_w, ffn_w, sc2, sh2, wr_t, seq):
    T, D = x2.shape
    d_hg, d_sb = o_hg.shape[1], o_sb.shape[1]
    E = wr_t.shape[0]
    tm = _pick(seq, (512, 256, 128))
    tk = _pick(d_hg, (512, 256, 128))
    assert d_sb % tk == 0 and (D // 2) % HEAD_DIM == 0
    n_hg, n_sb = d_hg // tk, d_sb // tk
    vec = pl.BlockSpec((1, D), lambda i, k: (0, 0))
    per_b = pl.BlockSpec((None, 1, D), lambda i, k: ((i * tm) // seq, 0, 0))
    row_blk = pl.BlockSpec((tm, D), lambda i, k: (i, 0))
    return pl.pallas_call(
        functools.partial(_outproj_kernel, n_hg=n_hg),
        grid=(T // tm, n_hg + n_sb),
        in_specs=[pl.BlockSpec((tm, tk), lambda i, k: (i, jnp.minimum(k, n_hg - 1))),
                  pl.BlockSpec((tm, tk), lambda i, k: (i, jnp.maximum(k - n_hg, 0))),
                  pl.BlockSpec((tk, D), lambda i, k: (k, 0)),
                  row_blk, per_b, vec, vec, per_b, per_b,
                  pl.BlockSpec((E, D), lambda i, k: (0, 0))],
        out_specs=[row_blk, pl.BlockSpec((tm, D // 2), lambda i, k: (i, 0)),
                   pl.BlockSpec((E, tm), lambda i, k: (0, i))],
        out_shape=[jax.ShapeDtypeStruct((T, D), F32), jax.ShapeDtypeStruct((T, D // 2), U32),
                   jax.ShapeDtypeStruct((E, T), F32)],
        compiler_params=_params("arbitrary", "arbitrary"),
        name="out_proj",
    )(o_hg, o_sb, w_out_bf, x2, g1, post_w.reshape(1, D), ffn_w.reshape(1, D), sc2, sh2, wr_t)


def _topk_kernel(lg_ref, b_ref, idx_ref, gate_ref):
    l = lg_ref[...] + b_ref[...]
    E = l.shape[0]
    eid = lax.broadcasted_iota(jnp.int32, l.shape, 0)
    tops, ids = [], []
    for _ in range(TOP_K):
        m = jnp.max(l, axis=0, keepdims=True)
        sel = jnp.min(jnp.where(l == m, eid, E), axis=0, keepdims=True)
        tops.append(m)
        ids.append(sel)
        l = jnp.where(eid == sel, -jnp.inf, l)
    ex = [jnp.exp(t - tops[0]) for t in tops]
    den = ex[0]
    for t in ex[1:]:
        den = den + t
    idx_ref[...] = jnp.concatenate(ids, axis=0)
    gate_ref[...] = jnp.concatenate([t / den for t in ex], axis=0)


def _topk(logits_t, b_router):
    E, T = logits_t.shape
    tt = _pick(T, (2048, 1024, 512, 256, 128))
    return pl.pallas_call(
        _topk_kernel,
        grid=(T // tt,),
        in_specs=[pl.BlockSpec((E, tt), lambda i: (0, i)), pl.BlockSpec((E, 1), lambda i: (0, 0))],
        out_specs=[pl.BlockSpec((TOP_K, tt), lambda i: (0, i)), pl.BlockSpec((TOP_K, tt), lambda i: (0, i))],
        out_shape=[jax.ShapeDtypeStruct((TOP_K, T), jnp.int32), jax.ShapeDtypeStruct((TOP_K, T), F32)],
        compiler_params=_params("arbitrary"),
        name="topk",
    )(logits_t, b_router.reshape(E, 1))


def _moe_up_kernel(be_ref, nv_ref, tok_ref, tokn_ref, h_ref, w_ref, b_ref, o_ref,
                   stage, xb_ref, acc_ref, sem, *, rows, per):
    i = pl.program_id(0)
    c = pl.program_id(1)
    nc = pl.num_programs(1)
    nv = nv_ref[0]
    valid = i < nv
    half = stage.shape[1]
    tk = xb_ref.shape[2]
    d_ff = acc_ref.shape[2]

    def row_copy(tok, r):
        return pltpu.make_async_copy(h_ref.at[pl.ds(tok, 1), :], stage.at[pl.ds(r, 1), :], sem)

    @pl.when((i == 0) & (c == 0))
    def _():
        def start(r, carry):
            row_copy(tok_ref[0, r], r).start()
            return carry
        lax.fori_loop(0, rows, start, 0)

    @pl.when((i <= nv) & (c == 0))
    def _():
        def wait(r, carry):
            row_copy(0, r).wait()
            return carry
        lax.fori_loop(0, rows, wait, 0, unroll=8)

    @pl.when(valid & (c == 0))
    def _():
        lo, hi = _unpack_bf16_pair(stage[...])
        n_half = half // tk
        for j in range(n_half):
            xb_ref[j] = lo[:, j * tk:(j + 1) * tk]
            xb_ref[n_half + j] = hi[:, j * tk:(j + 1) * tk]
        acc_ref[...] = jnp.zeros_like(acc_ref)

    @pl.when(valid)
    def _():
        xk = xb_ref[c]
        acc_ref[0] += _dot(xk, w_ref[:, :d_ff].astype(BF16))
        acc_ref[1] += _dot(xk, w_ref[:, d_ff:].astype(BF16))
        for u in range(per):
            r = c * per + u
            row_copy(tokn_ref[0, r], r).start()

    @pl.when(valid & (c == nc - 1))
    def _():
        glu = jnp.minimum(acc_ref[0] + b_ref[:, :d_ff], SWIGLU_LIMIT)
        lin = jnp.clip(acc_ref[1] + b_ref[:, d_ff:], -SWIGLU_LIMIT, SWIGLU_LIMIT)
        o_ref[...] = (glu * jax.nn.sigmoid(SWIGLU_ALPHA * glu) * (lin + 1.0)).astype(BF16)

    @pl.when(jnp.logical_not(valid) & (c == nc - 1))
    def _():
        o_ref[...] = jnp.zeros_like(o_ref)


def _moe_up(h2p, row_tok, w_gu, b_gu, blk_expert, n_valid, rows):
    n_rows = row_tok.shape[0]
    half = h2p.shape[1]
    D = 2 * half
    E, _, two_ff = w_gu.shape
    d_ff = two_ff // 2
    tk = _pick(half, (512, 256, 128))
    nc = D // tk
    nb = n_rows // rows
    assert rows % nc == 0
    tok3 = row_tok.reshape(nb, 1, rows)

    grid_spec = pltpu.PrefetchScalarGridSpec(
        num_scalar_prefetch=2,
        grid=(nb, nc),
        in_specs=[pl.BlockSpec((None, 1, rows), lambda i, c, be, nv: (i, 0, 0), memory_space=pltpu.SMEM),
                  pl.BlockSpec((None, 1, rows), lambda i, c, be, nv: (jnp.minimum(i + 1, nb - 1), 0, 0),
                               memory_space=pltpu.SMEM),
                  pl.BlockSpec(memory_space=pl.ANY),
                  pl.BlockSpec((None, tk, two_ff), lambda i, c, be, nv: (be[i], jnp.where(i < nv[0], c, nc - 1), 0)),
                  pl.BlockSpec((None, 1, two_ff), lambda i, c, be, nv: (be[i], 0, 0))],
        out_specs=pl.BlockSpec((rows, d_ff), lambda i, c, be, nv: (i, 0)),
        scratch_shapes=[pltpu.VMEM((rows, half), U32), pltpu.VMEM((nc, rows, tk), BF16),
                        pltpu.VMEM((2, rows, d_ff), F32), pltpu.SemaphoreType.DMA(())],
    )
    return pl.pallas_call(
        functools.partial(_moe_up_kernel, rows=rows, per=rows // nc),
        grid_spec=grid_spec,
        out_shape=jax.ShapeDtypeStruct((n_rows, d_ff), BF16),
        compiler_params=_params("arbitrary", "arbitrary"),
        name="moe_up",
    )(blk_expert, n_valid, tok3, tok3, h2p, w_gu, b_gu.reshape(E, 1, two_ff))


def _moe_down_kernel(be_ref, nv_ref, a_ref, w_ref, b_ref, o_ref):
    valid = pl.program_id(0) < nv_ref[0]

    @pl.when(valid)
    def _():
        o_ref[...] = _dot(a_ref[...], w_ref[...].astype(BF16)) + b_ref[...]

    @pl.when(jnp.logical_not(valid))
    def _():
        o_ref[...] = jnp.zeros_like(o_ref)


def _moe_down(act, w_down, b_down, blk_expert, n_valid, rows):
    n_rows, d_ff = act.shape
    E, _, D = w_down.shape
    tn = _pick(D, (1024, 512, 256, 128))
    nj = D // tn
    nb = n_rows // rows

    def col(i, j, nv):
        return jnp.where(i < nv[0], j, nj - 1)

    grid_spec = pltpu.PrefetchScalarGridSpec(
        num_scalar_prefetch=2,
        grid=(nb, nj),
        in_specs=[pl.BlockSpec((rows, d_ff), lambda i, j, be, nv: (jnp.minimum(i, nv[0] - 1), 0)),
                  pl.BlockSpec((None, d_ff, tn), lambda i, j, be, nv: (be[i], 0, col(i, j, nv))),
                  pl.BlockSpec((None, 1, tn), lambda i, j, be, nv: (be[i], 0, col(i, j, nv)))],
        out_specs=pl.BlockSpec((rows, tn), lambda i, j, be, nv: (i, j)),
    )
    return pl.pallas_call(
        _moe_down_kernel,
        grid_spec=grid_spec,
        out_shape=jax.ShapeDtypeStruct((n_rows, D), F32),
        compiler_params=_params("arbitrary", "arbitrary"),
        name="moe_down",
    )(blk_expert, n_valid, act, w_down, b_down.reshape(E, 1, D))


def _combine_kernel(pos_ref, posn_ref, gate_ref, y_ref, x1_ref, g2_ref, nw_ref, o_ref, buf, sem, *, tm):
    i = pl.program_id(0)
    slot = i % 2

    def row_copy(src_row, s, k, r):
        return pltpu.make_async_copy(y_ref.at[pl.ds(src_row, 1), :], buf.at[s, k, pl.ds(r, 1), :], sem.at[s])

    def wait_block(s):
        for k in range(TOP_K):
            def wait(r, carry, k=k):
                row_copy(0, s, k, r).wait()
                return carry
            lax.fori_loop(0, tm, wait, 0, unroll=8)

    @pl.when(i == 0)
    def _():
        for k in range(TOP_K):
            def start(r, carry, k=k):
                row_copy(pos_ref[k, r], 0, k, r).start()
                return carry
            lax.fori_loop(0, tm, start, 0)

    wait_block(slot)

    grp = 16
    for g0 in range(0, tm, grp):
        for k in range(TOP_K):
            for r in range(g0, g0 + grp):
                row_copy(posn_ref[k, r], 1 - slot, k, r).start()
        rs = slice(g0, g0 + grp)
        m = gate_ref[0, rs, :] * buf[slot, 0, rs, :]
        for k in range(1, TOP_K):
            m = m + gate_ref[k, rs, :] * buf[slot, k, rs, :]
        o_ref[rs, :] = x1_ref[rs, :] + g2_ref[...] * _rms(m, nw_ref[...])

    @pl.when(i == pl.num_programs(0) - 1)
    def _():
        wait_block(1 - slot)


def _moe_combine(y, pos, gates, x1, g2, norm_w, seq):
    T, D = x1.shape
    tm = _pick(seq, (128,))
    nt = T // tm
    pos_b = pos.reshape(TOP_K, nt, tm).transpose(1, 0, 2)
    return pl.pallas_call(
        functools.partial(_combine_kernel, tm=tm),
        grid=(nt,),
        in_specs=[pl.BlockSpec((None, TOP_K, tm), lambda i: (i, 0, 0), memory_space=pltpu.SMEM),
                  pl.BlockSpec((None, TOP_K, tm), lambda i: (jnp.minimum(i + 1, nt - 1), 0, 0),
                               memory_space=pltpu.SMEM),
                  pl.BlockSpec((TOP_K, tm, 1), lambda i: (0, i, 0)),
                  pl.BlockSpec(memory_space=pl.ANY),
                  pl.BlockSpec((tm, D), lambda i: (i, 0)),
                  pl.BlockSpec((None, 1, D), lambda i: ((i * tm) // seq, 0, 0)),
                  pl.BlockSpec((1, D), lambda i: (0, 0))],
        out_specs=pl.BlockSpec((tm, D), lambda i: (i, 0)),
        out_shape=jax.ShapeDtypeStruct((T, D), F32),
        scratch_shapes=[pltpu.VMEM((2, TOP_K, tm, D), F32), pltpu.SemaphoreType.DMA((2,))],
        compiler_params=_params("arbitrary"),
        name="moe_combine",
    )(pos_b, pos_b, gates.reshape(TOP_K, T, 1), y, x1, g2, norm_w.reshape(1, D))


def _routing_tables(top_idx, n_experts, rows):
    T = top_idx.shape[1]
    n_assign = T * TOP_K
    i32 = jnp.int32
    flat_e = top_idx.T.reshape(-1)
    iota = jnp.arange(n_assign, dtype=i32)
    _, order = lax.sort((flat_e, iota), num_keys=1)
    _, inv = lax.sort((order, iota), num_keys=1)
    eids = jnp.arange(n_experts, dtype=i32)
    counts = jnp.sum((flat_e[None, :] == eids[:, None]).astype(i32), axis=1)
    padded = (counts + rows - 1) // rows * rows
    pad_end = jnp.cumsum(padded)
    pad_start = pad_end - padded
    start = jnp.cumsum(counts) - counts
    n_rows = -(-n_assign // rows) * rows + n_experts * rows
    n_blocks = n_rows // rows
    pos = (pad_start[flat_e] + inv - start[flat_e]).astype(i32).reshape(T, TOP_K).T

    def expert_of(r):
        return jnp.minimum(jnp.sum((pad_end[None, :] <= r[:, None]).astype(i32), axis=1), n_experts - 1)

    r = jnp.arange(n_rows, dtype=i32)
    row_e = expert_of(r)
    off = r - pad_start[row_e]
    real = (off < counts[row_e]) & (r < pad_end[-1])
    src = jnp.clip(start[row_e] + off, 0, n_assign - 1)
    row_tok = jnp.where(real, order[src] // TOP_K, 0).astype(i32)
    n_valid = (pad_end[-1] // rows).astype(i32)
    blk = jnp.minimum(jnp.arange(n_blocks, dtype=i32), n_valid - 1) * rows
    return row_tok, pos, expert_of(blk).astype(i32), n_valid.reshape(1)


def kernel(x, c, w_mod, b_mod, mix_pre_norm, mix_post_norm, w_in, hg_lb, hg_norm_w, w_out, ffn_pre_norm,
           ffn_post_norm, w_router, b_router, w_gu, b_gu, w_down, b_down):
    B, S, D = x.shape
    depth = w_mod.shape[0]
    d_hg = hg_lb.shape[1]
    d_sb = w_out.shape[1] - d_hg
    E = w_router.shape[2]
    T = B * S
    assert depth == 1 and S % SB_BLOCK == 0 and d_hg % HEAD_DIM == 0 and d_sb % HEAD_DIM == 0
    moe_rows = MOE_ROWS

    x2 = x.reshape(T, D)
    for l in range(depth):
        mod = _modulation(c, w_mod[l], b_mod[l])
        sh1, sc1, g1, sh2, sc2, g2 = [mod[:, j * D:(j + 1) * D].reshape(B, 1, D) for j in range(6)]

        proj_hg, proj_sb = _in_proj(x2, mix_pre_norm[l], sc1, sh1, w_in[l].astype(BF16), S, 4 * d_hg)
        o_hg = _hgrn2(proj_hg, hg_lb, hg_norm_w[l], B, S, d_hg)
        o_sb = _stick_breaking(proj_sb, B, S, d_sb)
        x1, h2p, logits_t = _out_proj(o_hg, o_sb, w_out[l].astype(BF16), x2, g1, mix_post_norm[l],
                                      ffn_pre_norm[l], sc2, sh2, w_router[l].T.astype(BF16), S)

        top_idx, gates = _topk(logits_t, b_router[l])
        row_tok, pos, blk_expert, n_valid = _routing_tables(top_idx, E, moe_rows)
        act = _moe_up(h2p, row_tok, w_gu[l], b_gu[l], blk_expert, n_valid, moe_rows)
        y = _moe_down(act, w_down[l], b_down[l], blk_expert, n_valid, moe_rows)
        x2 = _moe_combine(y, pos, gates, x1, g2, ffn_post_norm[l], S)
    return x2.reshape(B, S, D)
```

```python
import functools

import jax
import jax.numpy as jnp
from jax import lax
from jax.experimental import pallas as pl
from jax.experimental.pallas import tpu as pltpu

HEAD_DIM = 128
HG_CHUNK = 64
HG_SUB = 16
SB_BLOCK = 128
TOP_K = 4
MOE_ROWS = 768
SWIGLU_ALPHA = 1.702
SWIGLU_LIMIT = 7.0
NORM_EPS = 1e-6
VMEM_LIMIT_BYTES = 56 * 1024 * 1024

F32 = jnp.float32
BF16 = jnp.bfloat16
U32 = jnp.uint32


def _pick(dim, prefs):
    for p in prefs:
        if dim % p == 0:
            return p
    return dim


def _params(*sem):
    return pltpu.CompilerParams(dimension_semantics=sem, vmem_limit_bytes=VMEM_LIMIT_BYTES)


def _rms(t, w):
    return t * lax.rsqrt(jnp.mean(t * t, axis=-1, keepdims=True) + NORM_EPS) * w


def _dot(a, b):
    return jnp.dot(a, b, preferred_element_type=F32)


def _dot_nt(a, b):
    return lax.dot_general(a, b, (((1,), (1,)), ((), ())), preferred_element_type=F32)


def _pack_bf16_pair(lo, hi):
    lo_bits = lax.bitcast_convert_type(lo.astype(BF16).astype(F32), U32) >> 16
    hi_bits = lax.bitcast_convert_type(hi.astype(BF16).astype(F32), U32) & jnp.uint32(0xFFFF0000)
    return hi_bits | lo_bits


def _unpack_bf16_pair(words):
    lo = lax.bitcast_convert_type(words << 16, F32).astype(BF16)
    hi = lax.bitcast_convert_type(words & jnp.uint32(0xFFFF0000), F32).astype(BF16)
    return lo, hi


def _mod_kernel(c_ref, w_ref, b_ref, o_ref):
    c = c_ref[...]
    a = (c * jax.nn.sigmoid(c)).astype(BF16)
    o_ref[...] = _dot(a, w_ref[...].astype(BF16)) + b_ref[...]


def _modulation(c, w_mod, b_mod):
    B, D = c.shape
    N = w_mod.shape[1]
    tn = _pick(N, (512, 256, 128))
    return pl.pallas_call(
        _mod_kernel,
        grid=(N // tn,),
        in_specs=[pl.BlockSpec((B, D), lambda j: (0, 0)),
                  pl.BlockSpec((D, tn), lambda j: (0, j)),
                  pl.BlockSpec((1, tn), lambda j: (0, j))],
        out_specs=pl.BlockSpec((B, tn), lambda j: (0, j)),
        out_shape=jax.ShapeDtypeStruct((B, N), F32),
        compiler_params=_params("arbitrary"),
        name="modulation",
    )(c, w_mod, b_mod.reshape(1, N))


def _inproj_kernel(x_ref, nw_ref, sc_ref, sh_ref, w_ref, o_ref, h_ref):
    @pl.when(pl.program_id(1) == 0)
    def _():
        h = _rms(x_ref[...], nw_ref[...]) * (1.0 + sc_ref[...]) + sh_ref[...]
        h_ref[...] = h.astype(BF16)

    o_ref[...] = _dot(h_ref[...], w_ref[...])


def _in_proj(x2, norm_w, sc, sh, w_bf, seq):
    T, D = x2.shape
    N = w_bf.shape[1]
    tm = _pick(seq, (512, 256, 128))
    tn = _pick(N, (1024, 512, 256, 128))
    return pl.pallas_call(
        _inproj_kernel,
        grid=(T // tm, N // tn),
        in_specs=[pl.BlockSpec((tm, D), lambda i, j: (i, 0)),
                  pl.BlockSpec((1, D), lambda i, j: (0, 0)),
                  pl.BlockSpec((None, 1, D), lambda i, j: ((i * tm) // seq, 0, 0)),
                  pl.BlockSpec((None, 1, D), lambda i, j: ((i * tm) // seq, 0, 0)),
                  pl.BlockSpec((D, tn), lambda i, j: (0, j))],
        out_specs=pl.BlockSpec((tm, tn), lambda i, j: (i, j)),
        out_shape=jax.ShapeDtypeStruct((T, N), F32),
        scratch_shapes=[pltpu.VMEM((tm, D), BF16)],
        compiler_params=_params("arbitrary", "arbitrary"),
        name="in_proj",
    )(x2, norm_w.reshape(1, D), sc, sh, w_bf)


def _split3(t):
    hi = t.astype(BF16)
    r1 = t - hi.astype(F32)
    mid = r1.astype(BF16)
    lo = (r1 - mid.astype(F32)).astype(BF16)
    return hi, mid, lo


def _hgrn_kernel(lb_ref, nw_ref, q_ref, f_ref, v_ref, g_ref, o_ref, st_ref, b_s, kk_s, v_s, od_s, *, n_chunks, heads):
    C, SUB = HG_CHUNK, HG_SUB

    @pl.when(pl.program_id(2) == 0)
    def _():
        st_ref[...] = jnp.zeros_like(st_ref)

    hg = lb_ref[...]
    e = jnp.exp(hg - jnp.max(hg, axis=0, keepdims=True))
    lb_all = e[0:1, :] / jnp.sum(e, axis=0, keepdims=True)
    nw = nw_ref[...]
    row = lax.broadcasted_iota(jnp.int32, (C, C), 0)
    col = lax.broadcasted_iota(jnp.int32, (C, C), 1)
    tri = jnp.where(row >= col, 1.0, 0.0).astype(BF16)
    rows = lax.broadcasted_iota(jnp.int32, (C, 1), 0)
    rows8 = lax.broadcasted_iota(jnp.int32, (8, 1), 0)

    def lanes(h):
        return slice(h * HEAD_DIM, (h + 1) * HEAD_DIM)

    def decay(r, h):
        lb = lb_all[:, lanes(h)]
        q = q_ref[pl.ds(r, C), lanes(h)]
        v = v_ref[pl.ds(r, C), lanes(h)]
        f = lb + (1.0 - lb) * jax.nn.sigmoid(f_ref[pl.ds(r, C), lanes(h)])
        kk = 1.0 - f
        hi, mid, lo = _split3(jnp.log(f))
        b = _dot(tri, hi) + _dot(tri, mid) + _dot(tri, lo)
        b_s[h] = b
        kk_s[h] = kk
        v_s[h] = v
        return q, v, kk, b

    def matmul_terms(h, q, v, kk, b):
        b_last = b_s[h, C - 1:C, :]
        st = st_ref[h]
        o = _dot_nt((q * jnp.exp(b)).astype(BF16), st.astype(BF16))
        p = jnp.zeros((C, C), F32)
        for i in range(1, C // SUB):
            bi = b_s[h, i * SUB - 1:i * SUB, :]
            in_i = (rows >= i * SUB) & (rows < (i + 1) * SUB)
            qh = jnp.where(in_i, q * jnp.exp(jnp.minimum(b - bi, 0.0)), 0.0)
            kh = jnp.where(rows < i * SUB, kk * jnp.exp(jnp.minimum(bi - b, 0.0)), 0.0)
            p = p + _dot_nt(qh.astype(BF16), kh.astype(BF16))
        kt = (kk * jnp.exp(b_last - b)).astype(BF16)
        st_ref[h] = st * jnp.exp(b_last) + _dot(v.T.astype(BF16), kt)
        return o, p

    def pairwise(h, q, b):
        for i in range(C // SUB):
            for g0 in range(0, SUB, 8):
                r0 = i * SUB + g0
                qi = q[r0:r0 + 8, :]
                bi = b[r0:r0 + 8, :]
                acc = jnp.zeros((8, HEAD_DIM), F32)
                for s in range(g0 + 8):
                    rs = i * SUB + s
                    a = qi * jnp.exp(bi - b_s[h, rs:rs + 1, :]) * kk_s[h, rs:rs + 1, :]
                    sc = jnp.sum(a, axis=-1, keepdims=True)
                    if s >= g0:
                        sc = jnp.where(rows8 >= s - g0, sc, 0.0)
                    acc = acc + sc * v_s[h, rs:rs + 1, :]
                od_s[h, r0:r0 + 8, :] = acc

    def chunk(c, carry):
        r = pl.multiple_of(c * C, C)
        hs = range(heads)
        qvkb = [decay(r, h) for h in hs]
        op = [matmul_terms(h, *qvkb[h]) for h in hs]
        for h in hs:
            pairwise(h, qvkb[h][0], qvkb[h][3])
        for h in hs:
            o, p = op[h]
            o = o + _dot(p.astype(BF16), qvkb[h][1].astype(BF16)) + od_s[h]
            g = g_ref[pl.ds(r, C), lanes(h)]
            o_ref[pl.ds(r, C), lanes(h)] = (_rms(o, nw) * (g * jax.nn.sigmoid(g))).astype(BF16)
        return carry

    lax.fori_loop(0, n_chunks, chunk, 0)


def _hgrn2(proj, hg_lb, hg_norm_w, batch, seq, d_hg):
    T = proj.shape[0]
    H = d_hg // HEAD_DIM
    G = _pick(H, (4, 2, 1))
    W = G * HEAD_DIM
    L = _pick(seq, (512, 256, 128, 64))
    nl = seq // L
    n_slots = hg_lb.shape[0]

    def col(off):
        return pl.BlockSpec((L, W), lambda b, h, s: (b * nl + s, off * (H // G) + h))

    return pl.pallas_call(
        functools.partial(_hgrn_kernel, n_chunks=L // HG_CHUNK, heads=G),
        grid=(batch, H // G, nl),
        in_specs=[pl.BlockSpec((n_slots, W), lambda b, h, s: (0, h)),
                  pl.BlockSpec((1, HEAD_DIM), lambda b, h, s: (0, 0)),
                  col(0), col(1), col(2), col(3)],
        out_specs=pl.BlockSpec((L, W), lambda b, h, s: (b * nl + s, h)),
        out_shape=jax.ShapeDtypeStruct((T, d_hg), BF16),
        scratch_shapes=[pltpu.VMEM((G, HEAD_DIM, HEAD_DIM), F32)] + [pltpu.VMEM((G, HG_CHUNK, HEAD_DIM), F32)] * 4,
        compiler_params=_params("arbitrary", "arbitrary", "arbitrary"),
        name="hgrn2",
    )(hg_lb, hg_norm_w.reshape(1, HEAD_DIM), proj, proj, proj, proj)


def _sb_kernel(q_ref, k_ref, v_ref, o_ref, acc_ref, tail_ref, *, heads, tq):
    TK = SB_BLOCK
    i = pl.program_id(2)
    n_kb = (i + 1) * (tq // TK)
    scale = HEAD_DIM ** -0.5
    row = lax.broadcasted_iota(jnp.int32, (tq, TK), 0)
    col = lax.broadcasted_iota(jnp.int32, (tq, TK), 1)
    r2 = lax.broadcasted_iota(jnp.int32, (2 * TK, 2 * TK), 0)
    c2 = lax.broadcasted_iota(jnp.int32, (2 * TK, 2 * TK), 1)
    r2 = jnp.where(r2 >= TK, r2 - TK, r2)
    op = jnp.where((c2 >= TK) | (r2 >= c2), 1.0, 0.0).astype(BF16)
    acc_ref[...] = jnp.zeros_like(acc_ref)
    tail_ref[...] = jnp.zeros_like(tail_ref)

    def body(jj, carry, *, masked):
        j = n_kb - 1 - jj
        r = pl.multiple_of(j * TK, TK)
        strict = (col + j * TK) < (row + i * tq)
        hs = range(heads)
        ls = [slice(h * HEAD_DIM, (h + 1) * HEAD_DIM) for h in hs]
        zs = [_dot_nt(q_ref[:, ls[h]].astype(BF16), k_ref[pl.ds(r, TK), ls[h]].astype(BF16)) * scale for h in hs]
        sums = []
        for z in zs:
            lk = -(jnp.maximum(z, 0.0) + jnp.log(1.0 + jnp.exp(-jnp.abs(z))))
            if masked:
                lk = jnp.where(strict, lk, 0.0)
            hi = lk.astype(BF16)
            lo = (lk - hi.astype(F32)).astype(BF16)
            sums.append(_dot(jnp.concatenate([hi, lo], axis=1), op))
        for h in hs:
            log_w = zs[h] + sums[h][:, :TK] + tail_ref[h]
            if masked:
                log_w = jnp.where(strict, log_w, -jnp.inf)
            acc_ref[h] += _dot(jnp.exp(log_w).astype(BF16), v_ref[pl.ds(r, TK), ls[h]].astype(BF16))
            tail_ref[h] += sums[h][:, TK:]
        return carry

    n_diag = tq // TK
    lax.fori_loop(0, n_diag, functools.partial(body, masked=True), 0)
    lax.fori_loop(n_diag, n_kb, functools.partial(body, masked=False), 0)
    for h in range(heads):
        o_ref[:, h * HEAD_DIM:(h + 1) * HEAD_DIM] = acc_ref[h].astype(BF16)


def _stick_breaking(proj, batch, seq, d_hg, d_sb):
    T = proj.shape[0]
    H = d_sb // HEAD_DIM
    G = _pick(H, (4, 2, 1))
    W = G * HEAD_DIM
    assert (4 * d_hg) % W == 0
    tq = _pick(seq, (256, 128))
    nq = seq // tq
    base = 4 * d_hg // W
    return pl.pallas_call(
        functools.partial(_sb_kernel, heads=G, tq=tq),
        grid=(batch, H // G, nq),
        in_specs=[pl.BlockSpec((tq, W), lambda b, h, i: (b * nq + i, base + h)),
                  pl.BlockSpec((seq, W), lambda b, h, i: (b, base + H // G + h)),
                  pl.BlockSpec((seq, W), lambda b, h, i: (b, base + 2 * (H // G) + h))],
        out_specs=pl.BlockSpec((tq, W), lambda b, h, i: (b * nq + i, h)),
        out_shape=jax.ShapeDtypeStruct((T, d_sb), BF16),
        scratch_shapes=[pltpu.VMEM((G, tq, HEAD_DIM), F32), pltpu.VMEM((G, tq, HEAD_DIM), F32)],
        compiler_params=_params("arbitrary", "arbitrary", "arbitrary"),
        name="stickbreak",
    )(proj, proj, proj)


def _outproj_kernel(ohg_ref, osb_ref, w_ref, x_ref, g1_ref, pw_ref, fw_ref, sc_ref, sh_ref, wr_ref,
                    x1_ref, h2p_ref, lg_ref, *, n_hg):
    k = pl.program_id(1)

    @pl.when(k == 0)
    def _():
        x1_ref[...] = jnp.zeros_like(x1_ref)

    @pl.when(k < n_hg)
    def _():
        x1_ref[...] += _dot(ohg_ref[...], w_ref[...])

    @pl.when(k >= n_hg)
    def _():
        x1_ref[...] += _dot(osb_ref[...], w_ref[...])

    @pl.when(k == pl.num_programs(1) - 1)
    def _():
        x1 = x_ref[...] + g1_ref[...] * _rms(x1_ref[...], pw_ref[...])
        x1_ref[...] = x1
        h2 = _rms(x1, fw_ref[...]) * (1.0 + sc_ref[...]) + sh_ref[...]
        half = h2.shape[1] // 2
        h2p_ref[...] = _pack_bf16_pair(h2[:, :half], h2[:, half:])
        lg_ref[...] = _dot_nt(wr_ref[...], h2.astype(BF16))


def _out_proj(o_hg, o_sb, w_out_bf, x2, g1, post_w, ffn_w, sc2, sh2, wr_t, seq):
    T, D = x2.shape
    d_hg, d_sb = o_hg.shape[1], o_sb.shape[1]
    E = wr_t.shape[0]
    tm = _pick(seq, (512, 256, 128))
    tk = _pick(d_hg, (512, 256, 128))
    assert d_sb % tk == 0 and (D // 2) % HEAD_DIM == 0
    n_hg, n_sb = d_hg // tk, d_sb // tk
    vec = pl.BlockSpec((1, D), lambda i, k: (0, 0))
    per_b = pl.BlockSpec((None, 1, D), lambda i, k: ((i * tm) // seq, 0, 0))
    row_blk = pl.BlockSpec((tm, D), lambda i, k: (i, 0))
    return pl.pallas_call(
        functools.partial(_outproj_kernel, n_hg=n_hg),
        grid=(T // tm, n_hg + n_sb),
        in_specs=[pl.BlockSpec((tm, tk), lambda i, k: (i, jnp.minimum(k, n_hg - 1))),
                  pl.BlockSpec((tm, tk), lambda i, k: (i, jnp.maximum(k - n_hg, 0))),
                  pl.BlockSpec((tk, D), lambda i, k: (k, 0)),
                  row_blk, per_b, vec, vec, per_b, per_b,
                  pl.BlockSpec((E, D), lambda i, k: (0, 0))],
        out_specs=[row_blk, pl.BlockSpec((tm, D // 2), lambda i, k: (i, 0)),
                   pl.BlockSpec((E, tm), lambda i, k: (0, i))],
        out_shape=[jax.ShapeDtypeStruct((T, D), F32), jax.ShapeDtypeStruct((T, D // 2), U32),
                   jax.ShapeDtypeStruct((E, T), F32)],
        compiler_params=_params("arbitrary", "arbitrary"),
        name="out_proj",
    )(o_hg, o_sb, w_out_bf, x2, g1, post_w.reshape(1, D), ffn_w.reshape(1, D), sc2, sh2, wr_t)


def _topk_kernel(lg_ref, b_ref, idx_ref, gate_ref):
    l = lg_ref[...] + b_ref[...]
    E = l.shape[0]
    eid = lax.broadcasted_iota(jnp.int32, l.shape, 0)
    tops, ids = [], []
    for _ in range(TOP_K):
        m = jnp.max(l, axis=0, keepdims=True)
        sel = jnp.min(jnp.where(l == m, eid, E), axis=0, keepdims=True)
        tops.append(m)
        ids.append(sel)
        l = jnp.where(eid == sel, -jnp.inf, l)
    ex = [jnp.exp(t - tops[0]) for t in tops]
    den = ex[0]
    for t in ex[1:]:
        den = den + t
    idx_ref[...] = jnp.concatenate(ids, axis=0)
    gate_ref[...] = jnp.concatenate([t / den for t in ex], axis=0)


def _topk(logits_t, b_router):
    E, T = logits_t.shape
    tt = _pick(T, (2048, 1024, 512, 256, 128))
    return pl.pallas_call(
        _topk_kernel,
        grid=(T // tt,),
        in_specs=[pl.BlockSpec((E, tt), lambda i: (0, i)), pl.BlockSpec((E, 1), lambda i: (0, 0))],
        out_specs=[pl.BlockSpec((TOP_K, tt), lambda i: (0, i)), pl.BlockSpec((TOP_K, tt), lambda i: (0, i))],
        out_shape=[jax.ShapeDtypeStruct((TOP_K, T), jnp.int32), jax.ShapeDtypeStruct((TOP_K, T), F32)],
        compiler_params=_params("arbitrary"),
        name="topk",
    )(logits_t, b_router.reshape(E, 1))


def _moe_up_kernel(be_ref, nv_ref, tok_ref, tokn_ref, h_ref, wg_ref, wl_ref, bg_ref, bl_ref, o_ref,
                   stage, xb_ref, sem, *, rows, per):
    i = pl.program_id(0)
    c = pl.program_id(1)
    nv = nv_ref[0]
    valid = i < nv
    half = stage.shape[1]

    def row_copy(tok, r):
        return pltpu.make_async_copy(h_ref.at[pl.ds(tok, 1), :], stage.at[pl.ds(r, 1), :], sem)

    @pl.when((i == 0) & (c == 0))
    def _():
        def start(r, carry):
            row_copy(tok_ref[0, r], r).start()
            return carry
        lax.fori_loop(0, rows, start, 0)

    @pl.when((i <= nv) & (c == 0))
    def _():
        def wait(r, carry):
            row_copy(0, r).wait()
            return carry
        lax.fori_loop(0, rows, wait, 0, unroll=8)

    @pl.when(valid & (c == 0))
    def _():
        lo, hi = _unpack_bf16_pair(stage[...])
        xb_ref[:, :half] = lo
        xb_ref[:, half:] = hi

    @pl.when(valid)
    def _():
        xb = xb_ref[...]
        glu = _dot(xb, wg_ref[...].astype(BF16)) + bg_ref[...]
        lin = _dot(xb, wl_ref[...].astype(BF16)) + bl_ref[...]
        glu = jnp.minimum(glu, SWIGLU_LIMIT)
        lin = jnp.clip(lin, -SWIGLU_LIMIT, SWIGLU_LIMIT)
        o_ref[...] = (glu * jax.nn.sigmoid(SWIGLU_ALPHA * glu) * (lin + 1.0)).astype(BF16)
        for u in range(per):
            r = c * per + u
            row_copy(tokn_ref[0, r], r).start()

    @pl.when(jnp.logical_not(valid))
    def _():
        o_ref[...] = jnp.zeros_like(o_ref)


def _moe_up(h2p, row_tok, w_gu, b_gu, blk_expert, n_valid, rows):
    n_rows = row_tok.shape[0]
    half = h2p.shape[1]
    D = 2 * half
    E, _, two_ff = w_gu.shape
    d_ff = two_ff // 2
    tf = _pick(d_ff, (256, 128))
    nc = d_ff // tf
    nb = n_rows // rows
    assert rows % nc == 0
    tok3 = row_tok.reshape(nb, 1, rows)

    def chunk(i, c, nv):
        return jnp.where(i < nv[0], c, nc - 1)

    grid_spec = pltpu.PrefetchScalarGridSpec(
        num_scalar_prefetch=2,
        grid=(nb, nc),
        in_specs=[pl.BlockSpec((None, 1, rows), lambda i, c, be, nv: (i, 0, 0), memory_space=pltpu.SMEM),
                  pl.BlockSpec((None, 1, rows), lambda i, c, be, nv: (jnp.minimum(i + 1, nb - 1), 0, 0),
                               memory_space=pltpu.SMEM),
                  pl.BlockSpec(memory_space=pl.ANY),
                  pl.BlockSpec((None, D, tf), lambda i, c, be, nv: (be[i], 0, chunk(i, c, nv))),
                  pl.BlockSpec((None, D, tf), lambda i, c, be, nv: (be[i], 0, nc + chunk(i, c, nv))),
                  pl.BlockSpec((None, 1, tf), lambda i, c, be, nv: (be[i], 0, chunk(i, c, nv))),
                  pl.BlockSpec((None, 1, tf), lambda i, c, be, nv: (be[i], 0, nc + chunk(i, c, nv)))],
        out_specs=pl.BlockSpec((rows, tf), lambda i, c, be, nv: (i, c)),
        scratch_shapes=[pltpu.VMEM((rows, half), U32), pltpu.VMEM((rows, D), BF16), pltpu.SemaphoreType.DMA(())],
    )
    return pl.pallas_call(
        functools.partial(_moe_up_kernel, rows=rows, per=rows // nc),
        grid_spec=grid_spec,
        out_shape=jax.ShapeDtypeStruct((n_rows, d_ff), BF16),
        compiler_params=_params("arbitrary", "arbitrary"),
        name="moe_up",
    )(blk_expert, n_valid, tok3, tok3, h2p, w_gu, w_gu, b_gu.reshape(E, 1, two_ff), b_gu.reshape(E, 1, two_ff))


def _moe_down_kernel(be_ref, nv_ref, a_ref, w_ref, b_ref, o_ref):
    valid = pl.program_id(0) < nv_ref[0]

    @pl.when(valid)
    def _():
        o_ref[...] = _dot(a_ref[...], w_ref[...].astype(BF16)) + b_ref[...]

    @pl.when(jnp.logical_not(valid))
    def _():
        o_ref[...] = jnp.zeros_like(o_ref)


def _moe_down(act, w_down, b_down, blk_expert, n_valid, rows):
    n_rows, d_ff = act.shape
    E, _, D = w_down.shape
    tn = _pick(D, (1024, 512, 256, 128))
    nj = D // tn
    nb = n_rows // rows

    def col(i, j, nv):
        return jnp.where(i < nv[0], j, nj - 1)

    grid_spec = pltpu.PrefetchScalarGridSpec(
        num_scalar_prefetch=2,
        grid=(nb, nj),
        in_specs=[pl.BlockSpec((rows, d_ff), lambda i, j, be, nv: (jnp.minimum(i, nv[0] - 1), 0)),
                  pl.BlockSpec((None, d_ff, tn), lambda i, j, be, nv: (be[i], 0, col(i, j, nv))),
                  pl.BlockSpec((None, 1, tn), lambda i, j, be, nv: (be[i], 0, col(i, j, nv)))],
        out_specs=pl.BlockSpec((rows, tn), lambda i, j, be, nv: (i, j)),
    )
    return pl.pallas_call(
        _moe_down_kernel,
        grid_spec=grid_spec,
        out_shape=jax.ShapeDtypeStruct((n_rows, D), F32),
        compiler_params=_params("arbitrary", "arbitrary"),
        name="moe_down",
    )(blk_expert, n_valid, act, w_down, b_down.reshape(E, 1, D))


def _combine_kernel(pos_ref, posn_ref, gate_ref, y_ref, x1_ref, g2_ref, nw_ref, o_ref, buf, sem, *, tm):
    i = pl.program_id(0)
    slot = i % 2

    def row_copy(src_row, s, k, r):
        return pltpu.make_async_copy(y_ref.at[pl.ds(src_row, 1), :], buf.at[s, k, pl.ds(r, 1), :], sem.at[s])

    def wait_block(s):
        for k in range(TOP_K):
            def wait(r, carry, k=k):
                row_copy(0, s, k, r).wait()
                return carry
            lax.fori_loop(0, tm, wait, 0, unroll=8)

    @pl.when(i == 0)
    def _():
        for k in range(TOP_K):
            def start(r, carry, k=k):
                row_copy(pos_ref[k, r], 0, k, r).start()
                return carry
            lax.fori_loop(0, tm, start, 0)

    wait_block(slot)

    grp = 16
    for g0 in range(0, tm, grp):
        for k in range(TOP_K):
            for r in range(g0, g0 + grp):
                row_copy(posn_ref[k, r], 1 - slot, k, r).start()
        rs = slice(g0, g0 + grp)
        m = gate_ref[0, rs, :] * buf[slot, 0, rs, :]
        for k in range(1, TOP_K):
            m = m + gate_ref[k, rs, :] * buf[slot, k, rs, :]
        o_ref[rs, :] = x1_ref[rs, :] + g2_ref[...] * _rms(m, nw_ref[...])

    @pl.when(i == pl.num_programs(0) - 1)
    def _():
        wait_block(1 - slot)


def _moe_combine(y, pos, gates, x1, g2, norm_w, seq):
    T, D = x1.shape
    tm = _pick(seq, (128,))
    nt = T // tm
    pos_b = pos.reshape(TOP_K, nt, tm).transpose(1, 0, 2)
    return pl.pallas_call(
        functools.partial(_combine_kernel, tm=tm),
        grid=(nt,),
        in_specs=[pl.BlockSpec((None, TOP_K, tm), lambda i: (i, 0, 0), memory_space=pltpu.SMEM),
                  pl.BlockSpec((None, TOP_K, tm), lambda i: (jnp.minimum(i + 1, nt - 1), 0, 0),
                               memory_space=pltpu.SMEM),
                  pl.BlockSpec((TOP_K, tm, 1), lambda i: (0, i, 0)),
                  pl.BlockSpec(memory_space=pl.ANY),
                  pl.BlockSpec((tm, D), lambda i: (i, 0)),
                  pl.BlockSpec((None, 1, D), lambda i: ((i * tm) // seq, 0, 0)),
                  pl.BlockSpec((1, D), lambda i: (0, 0))],
        out_specs=pl.BlockSpec((tm, D), lambda i: (i, 0)),
        out_shape=jax.ShapeDtypeStruct((T, D), F32),
        scratch_shapes=[pltpu.VMEM((2, TOP_K, tm, D), F32), pltpu.SemaphoreType.DMA((2,))],
        compiler_params=_params("arbitrary"),
        name="moe_combine",
    )(pos_b, pos_b, gates.reshape(TOP_K, T, 1), y, x1, g2, norm_w.reshape(1, D))


def _routing_tables(top_idx, n_experts, rows):
    T = top_idx.shape[1]
    n_assign = T * TOP_K
    i32 = jnp.int32
    flat_e = top_idx.T.reshape(-1)
    iota = jnp.arange(n_assign, dtype=i32)
    _, order = lax.sort((flat_e, iota), num_keys=1)
    _, inv = lax.sort((order, iota), num_keys=1)
    eids = jnp.arange(n_experts, dtype=i32)
    counts = jnp.sum((flat_e[None, :] == eids[:, None]).astype(i32), axis=1)
    padded = (counts + rows - 1) // rows * rows
    pad_end = jnp.cumsum(padded)
    pad_start = pad_end - padded
    start = jnp.cumsum(counts) - counts
    n_rows = -(-n_assign // rows) * rows + n_experts * rows
    n_blocks = n_rows // rows
    pos = (pad_start[flat_e] + inv - start[flat_e]).astype(i32).reshape(T, TOP_K).T

    def expert_of(r):
        return jnp.minimum(jnp.sum((pad_end[None, :] <= r[:, None]).astype(i32), axis=1), n_experts - 1)

    r = jnp.arange(n_rows, dtype=i32)
    row_e = expert_of(r)
    off = r - pad_start[row_e]
    real = (off < counts[row_e]) & (r < pad_end[-1])
    src = jnp.clip(start[row_e] + off, 0, n_assign - 1)
    row_tok = jnp.where(real, order[src] // TOP_K, 0).astype(i32)
    n_valid = (pad_end[-1] // rows).astype(i32)
    blk = jnp.minimum(jnp.arange(n_blocks, dtype=i32), n_valid - 1) * rows
    return row_tok, pos, expert_of(blk).astype(i32), n_valid.reshape(1)


def kernel(x, c, w_mod, b_mod, mix_pre_norm, mix_post_norm, w_in, hg_lb, hg_norm_w, w_out, ffn_pre_norm,
           ffn_post_norm, w_router, b_router, w_gu, b_gu, w_down, b_down):
    B, S, D = x.shape
    depth = w_mod.shape[0]
    d_hg = hg_lb.shape[1]
    d_sb = w_out.shape[1] - d_hg
    E = w_router.shape[2]
    T = B * S
    assert depth == 1 and S % SB_BLOCK == 0 and d_hg % HEAD_DIM == 0 and d_sb % HEAD_DIM == 0
    moe_rows = MOE_ROWS

    x2 = x.reshape(T, D)
    for l in range(depth):
        mod = _modulation(c, w_mod[l], b_mod[l])
        sh1, sc1, g1, sh2, sc2, g2 = [mod[:, j * D:(j + 1) * D].reshape(B, 1, D) for j in range(6)]

        proj = _in_proj(x2, mix_pre_norm[l], sc1, sh1, w_in[l].astype(BF16), S)
        o_hg = _hgrn2(proj, hg_lb, hg_norm_w[l], B, S, d_hg)
        o_sb = _stick_breaking(proj, B, S, d_hg, d_sb)
        x1, h2p, logits_t = _out_proj(o_hg, o_sb, w_out[l].astype(BF16), x2, g1, mix_post_norm[l],
                                      ffn_pre_norm[l], sc2, sh2, w_router[l].T.astype(BF16), S)

        top_idx, gates = _topk(logits_t, b_router[l])
        row_tok, pos, blk_expert, n_valid = _routing_tables(top_idx, E, moe_rows)
        act = _moe_up(h2p, row_tok, w_gu[l], b_gu[l], blk_expert, n_valid, moe_rows)
        y = _moe_down(act, w_down[l], b_down[l], blk_expert, n_valid, moe_rows)
        x2 = _moe_combine(y, pos, gates, x1, g2, ffn_post_norm[l], S)
    return x2.reshape(B, S, D)
```

```python
import functools

import jax
import jax.numpy as jnp
from jax import lax
from jax.experimental import pallas as pl
from jax.experimental.pallas import tpu as pltpu

HEAD_DIM = 128
HG_CHUNK = 64
HG_SUB = 16
SB_BLOCK = 128
TOP_K = 4
MOE_ROWS = 768
SWIGLU_ALPHA = 1.702
SWIGLU_LIMIT = 7.0
NORM_EPS = 1e-6
VMEM_LIMIT_BYTES = 56 * 1024 * 1024

F32 = jnp.float32
BF16 = jnp.bfloat16
U32 = jnp.uint32


def _pick(dim, prefs):
    for p in prefs:
        if dim % p == 0:
            return p
    return dim


def _params(*sem):
    return pltpu.CompilerParams(dimension_semantics=sem, vmem_limit_bytes=VMEM_LIMIT_BYTES)


def _rms(t, w):
    return t * lax.rsqrt(jnp.mean(t * t, axis=-1, keepdims=True) + NORM_EPS) * w


def _dot(a, b):
    return jnp.dot(a, b, preferred_element_type=F32)


def _dot_nt(a, b):
    return lax.dot_general(a, b, (((1,), (1,)), ((), ())), preferred_element_type=F32)


def _pack_bf16_pair(lo, hi):
    lo_bits = lax.bitcast_convert_type(lo.astype(BF16).astype(F32), U32) >> 16
    hi_bits = lax.bitcast_convert_type(hi.astype(BF16).astype(F32), U32) & jnp.uint32(0xFFFF0000)
    return hi_bits | lo_bits


def _unpack_bf16_pair(words):
    lo = lax.bitcast_convert_type(words << 16, F32).astype(BF16)
    hi = lax.bitcast_convert_type(words & jnp.uint32(0xFFFF0000), F32).astype(BF16)
    return lo, hi


def _mod_kernel(c_ref, w_ref, b_ref, o_ref):
    c = c_ref[...]
    a = (c * jax.nn.sigmoid(c)).astype(BF16)
    o_ref[...] = _dot(a, w_ref[...].astype(BF16)) + b_ref[...]


def _modulation(c, w_mod, b_mod):
    B, D = c.shape
    N = w_mod.shape[1]
    tn = _pick(N, (512, 256, 128))
    return pl.pallas_call(
        _mod_kernel,
        grid=(N // tn,),
        in_specs=[pl.BlockSpec((B, D), lambda j: (0, 0)),
                  pl.BlockSpec((D, tn), lambda j: (0, j)),
                  pl.BlockSpec((1, tn), lambda j: (0, j))],
        out_specs=pl.BlockSpec((B, tn), lambda j: (0, j)),
        out_shape=jax.ShapeDtypeStruct((B, N), F32),
        compiler_params=_params("arbitrary"),
        name="modulation",
    )(c, w_mod, b_mod.reshape(1, N))


def _inproj_kernel(x_ref, nw_ref, sc_ref, sh_ref, w_ref, o_ref, h_ref):
    @pl.when(pl.program_id(1) == 0)
    def _():
        h = _rms(x_ref[...], nw_ref[...]) * (1.0 + sc_ref[...]) + sh_ref[...]
        h_ref[...] = h.astype(BF16)

    o_ref[...] = _dot(h_ref[...], w_ref[...])


def _in_proj(x2, norm_w, sc, sh, w_bf, seq):
    T, D = x2.shape
    N = w_bf.shape[1]
    tm = _pick(seq, (512, 256, 128))
    tn = _pick(N, (1024, 512, 256, 128))
    return pl.pallas_call(
        _inproj_kernel,
        grid=(T // tm, N // tn),
        in_specs=[pl.BlockSpec((tm, D), lambda i, j: (i, 0)),
                  pl.BlockSpec((1, D), lambda i, j: (0, 0)),
                  pl.BlockSpec((None, 1, D), lambda i, j: ((i * tm) // seq, 0, 0)),
                  pl.BlockSpec((None, 1, D), lambda i, j: ((i * tm) // seq, 0, 0)),
                  pl.BlockSpec((D, tn), lambda i, j: (0, j))],
        out_specs=pl.BlockSpec((tm, tn), lambda i, j: (i, j)),
        out_shape=jax.ShapeDtypeStruct((T, N), F32),
        scratch_shapes=[pltpu.VMEM((tm, D), BF16)],
        compiler_params=_params("arbitrary", "arbitrary"),
        name="in_proj",
    )(x2, norm_w.reshape(1, D), sc, sh, w_bf)


def _split3(t):
    hi = t.astype(BF16)
    r1 = t - hi.astype(F32)
    mid = r1.astype(BF16)
    lo = (r1 - mid.astype(F32)).astype(BF16)
    return hi, mid, lo


def _hgrn_kernel(lb_ref, nw_ref, q_ref, f_ref, v_ref, g_ref, o_ref, st_ref, b_s, kk_s, v_s, od_s, *, n_chunks, heads):
    C, SUB = HG_CHUNK, HG_SUB

    @pl.when(pl.program_id(2) == 0)
    def _():
        st_ref[...] = jnp.zeros_like(st_ref)

    hg = lb_ref[...]
    e = jnp.exp(hg - jnp.max(hg, axis=0, keepdims=True))
    lb_all = e[0:1, :] / jnp.sum(e, axis=0, keepdims=True)
    nw = nw_ref[...]
    row = lax.broadcasted_iota(jnp.int32, (C, C), 0)
    col = lax.broadcasted_iota(jnp.int32, (C, C), 1)
    tri = jnp.where(row >= col, 1.0, 0.0).astype(BF16)
    rows = lax.broadcasted_iota(jnp.int32, (C, 1), 0)
    rows8 = lax.broadcasted_iota(jnp.int32, (8, 1), 0)

    def lanes(h):
        return slice(h * HEAD_DIM, (h + 1) * HEAD_DIM)

    def decay(r, h):
        lb = lb_all[:, lanes(h)]
        q = q_ref[pl.ds(r, C), lanes(h)]
        v = v_ref[pl.ds(r, C), lanes(h)]
        f = lb + (1.0 - lb) * jax.nn.sigmoid(f_ref[pl.ds(r, C), lanes(h)])
        kk = 1.0 - f
        hi, mid, lo = _split3(jnp.log(f))
        b = _dot(tri, hi) + _dot(tri, mid) + _dot(tri, lo)
        b_s[h] = b
        kk_s[h] = kk
        v_s[h] = v
        return q, v, kk, b

    def matmul_terms(h, q, v, kk, b):
        b_last = b_s[h, C - 1:C, :]
        st = st_ref[h]
        o = _dot_nt((q * jnp.exp(b)).astype(BF16), st.astype(BF16))
        p = jnp.zeros((C, C), F32)
        for i in range(1, C // SUB):
            bi = b_s[h, i * SUB - 1:i * SUB, :]
            in_i = (rows >= i * SUB) & (rows < (i + 1) * SUB)
            qh = jnp.where(in_i, q * jnp.exp(jnp.minimum(b - bi, 0.0)), 0.0)
            kh = jnp.where(rows < i * SUB, kk * jnp.exp(jnp.minimum(bi - b, 0.0)), 0.0)
            p = p + _dot_nt(qh.astype(BF16), kh.astype(BF16))
        kt = (kk * jnp.exp(b_last - b)).astype(BF16)
        st_ref[h] = st * jnp.exp(b_last) + _dot(v.T.astype(BF16), kt)
        return o, p

    def pairwise(h, q, b):
        for i in range(C // SUB):
            for g0 in range(0, SUB, 8):
                r0 = i * SUB + g0
                qi = q[r0:r0 + 8, :]
                bi = b[r0:r0 + 8, :]
                acc = jnp.zeros((8, HEAD_DIM), F32)
                for s in range(g0 + 8):
                    rs = i * SUB + s
                    a = qi * jnp.exp(bi - b_s[h, rs:rs + 1, :]) * kk_s[h, rs:rs + 1, :]
                    sc = jnp.sum(a, axis=-1, keepdims=True)
                    if s >= g0:
                        sc = jnp.where(rows8 >= s - g0, sc, 0.0)
                    acc = acc + sc * v_s[h, rs:rs + 1, :]
                od_s[h, r0:r0 + 8, :] = acc

    def chunk(c, carry):
        r = pl.multiple_of(c * C, C)
        hs = range(heads)
        qvkb = [decay(r, h) for h in hs]
        op = [matmul_terms(h, *qvkb[h]) for h in hs]
        for h in hs:
            pairwise(h, qvkb[h][0], qvkb[h][3])
        for h in hs:
            o, p = op[h]
            o = o + _dot(p.astype(BF16), qvkb[h][1].astype(BF16)) + od_s[h]
            g = g_ref[pl.ds(r, C), lanes(h)]
            o_ref[pl.ds(r, C), lanes(h)] = (_rms(o, nw) * (g * jax.nn.sigmoid(g))).astype(BF16)
        return carry

    lax.fori_loop(0, n_chunks, chunk, 0)


def _hgrn2(proj, hg_lb, hg_norm_w, batch, seq, d_hg):
    T = proj.shape[0]
    H = d_hg // HEAD_DIM
    G = _pick(H, (4, 2, 1))
    W = G * HEAD_DIM
    L = _pick(seq, (512, 256, 128, 64))
    nl = seq // L
    n_slots = hg_lb.shape[0]

    def col(off):
        return pl.BlockSpec((L, W), lambda b, h, s: (b * nl + s, off * (H // G) + h))

    return pl.pallas_call(
        functools.partial(_hgrn_kernel, n_chunks=L // HG_CHUNK, heads=G),
        grid=(batch, H // G, nl),
        in_specs=[pl.BlockSpec((n_slots, W), lambda b, h, s: (0, h)),
                  pl.BlockSpec((1, HEAD_DIM), lambda b, h, s: (0, 0)),
                  col(0), col(1), col(2), col(3)],
        out_specs=pl.BlockSpec((L, W), lambda b, h, s: (b * nl + s, h)),
        out_shape=jax.ShapeDtypeStruct((T, d_hg), BF16),
        scratch_shapes=[pltpu.VMEM((G, HEAD_DIM, HEAD_DIM), F32)] + [pltpu.VMEM((G, HG_CHUNK, HEAD_DIM), F32)] * 4,
        compiler_params=_params("arbitrary", "arbitrary", "arbitrary"),
        name="hgrn2",
    )(hg_lb, hg_norm_w.reshape(1, HEAD_DIM), proj, proj, proj, proj)


def _sb_kernel(q_ref, k_ref, v_ref, o_ref, acc_ref, tail_ref, *, heads, tq):
    TK = SB_BLOCK
    i = pl.program_id(2)
    n_kb = (i + 1) * (tq // TK)
    scale = HEAD_DIM ** -0.5
    row = lax.broadcasted_iota(jnp.int32, (tq, TK), 0)
    col = lax.broadcasted_iota(jnp.int32, (tq, TK), 1)
    r2 = lax.broadcasted_iota(jnp.int32, (2 * TK, 2 * TK), 0)
    c2 = lax.broadcasted_iota(jnp.int32, (2 * TK, 2 * TK), 1)
    r2 = jnp.where(r2 >= TK, r2 - TK, r2)
    op = jnp.where((c2 >= TK) | (r2 >= c2), 1.0, 0.0).astype(BF16)
    acc_ref[...] = jnp.zeros_like(acc_ref)
    tail_ref[...] = jnp.zeros_like(tail_ref)

    def body(jj, carry, *, masked):
        j = n_kb - 1 - jj
        r = pl.multiple_of(j * TK, TK)
        strict = (col + j * TK) < (row + i * tq)
        hs = range(heads)
        ls = [slice(h * HEAD_DIM, (h + 1) * HEAD_DIM) for h in hs]
        zs = [_dot_nt(q_ref[:, ls[h]].astype(BF16), k_ref[pl.ds(r, TK), ls[h]].astype(BF16)) * scale for h in hs]
        sums = []
        for z in zs:
            lk = -(jnp.maximum(z, 0.0) + jnp.log(1.0 + jnp.exp(-jnp.abs(z))))
            if masked:
                lk = jnp.where(strict, lk, 0.0)
            hi = lk.astype(BF16)
            lo = (lk - hi.astype(F32)).astype(BF16)
            sums.append(_dot(jnp.concatenate([hi, lo], axis=1), op))
        for h in hs:
            log_w = zs[h] + sums[h][:, :TK] + tail_ref[h]
            if masked:
                log_w = jnp.where(strict, log_w, -jnp.inf)
            acc_ref[h] += _dot(jnp.exp(log_w).astype(BF16), v_ref[pl.ds(r, TK), ls[h]].astype(BF16))
            tail_ref[h] += sums[h][:, TK:]
        return carry

    n_diag = tq // TK
    lax.fori_loop(0, n_diag, functools.partial(body, masked=True), 0)
    lax.fori_loop(n_diag, n_kb, functools.partial(body, masked=False), 0)
    for h in range(heads):
        o_ref[:, h * HEAD_DIM:(h + 1) * HEAD_DIM] = acc_ref[h].astype(BF16)


def _stick_breaking(proj, batch, seq, d_hg, d_sb):
    T = proj.shape[0]
    H = d_sb // HEAD_DIM
    G = _pick(H, (4, 2, 1))
    W = G * HEAD_DIM
    assert (4 * d_hg) % W == 0
    tq = _pick(seq, (256, 128))
    nq = seq // tq
    base = 4 * d_hg // W
    return pl.pallas_call(
        functools.partial(_sb_kernel, heads=G, tq=tq),
        grid=(batch, H // G, nq),
        in_specs=[pl.BlockSpec((tq, W), lambda b, h, i: (b * nq + i, base + h)),
                  pl.BlockSpec((seq, W), lambda b, h, i: (b, base + H // G + h)),
                  pl.BlockSpec((seq, W), lambda b, h, i: (b, base + 2 * (H // G) + h))],
        out_specs=pl.BlockSpec((tq, W), lambda b, h, i: (b * nq + i, h)),
        out_shape=jax.ShapeDtypeStruct((T, d_sb), BF16),
        scratch_shapes=[pltpu.VMEM((G, tq, HEAD_DIM), F32), pltpu.VMEM((G, tq, HEAD_DIM), F32)],
        compiler_params=_params("arbitrary", "arbitrary", "arbitrary"),
        name="stickbreak",
    )(proj, proj, proj)


def _outproj_kernel(ohg_ref, osb_ref, w_ref, x_ref, g1_ref, pw_ref, fw_ref, sc_ref, sh_ref, wr_ref,
                    x1_ref, h2p_ref, lg_ref, *, n_hg):
    k = pl.program_id(1)

    @pl.when(k == 0)
    def _():
        x1_ref[...] = jnp.zeros_like(x1_ref)

    @pl.when(k < n_hg)
    def _():
        x1_ref[...] += _dot(ohg_ref[...], w_ref[...])

    @pl.when(k >= n_hg)
    def _():
        x1_ref[...] += _dot(osb_ref[...], w_ref[...])

    @pl.when(k == pl.num_programs(1) - 1)
    def _():
        x1 = x_ref[...] + g1_ref[...] * _rms(x1_ref[...], pw_ref[...])
        x1_ref[...] = x1
        h2 = _rms(x1, fw_ref[...]) * (1.0 + sc_ref[...]) + sh_ref[...]
        half = h2.shape[1] // 2
        h2p_ref[...] = _pack_bf16_pair(h2[:, :half], h2[:, half:])
        lg_ref[...] = _dot_nt(wr_ref[...], h2.astype(BF16))


def _out_proj(o_hg, o_sb, w_out_bf, x2, g1, post_w, ffn_w, sc2, sh2, wr_t, seq):
    T, D = x2.shape
    d_hg, d_sb = o_hg.shape[1], o_sb.shape[1]
    E = wr_t.shape[0]
    tm = _pick(seq, (512, 256, 128))
    tk = _pick(d_hg, (512, 256, 128))
    assert d_sb % tk == 0 and (D // 2) % HEAD_DIM == 0
    n_hg, n_sb = d_hg // tk, d_sb // tk
    vec = pl.BlockSpec((1, D), lambda i, k: (0, 0))
    per_b = pl.BlockSpec((None, 1, D), lambda i, k: ((i * tm) // seq, 0, 0))
    row_blk = pl.BlockSpec((tm, D), lambda i, k: (i, 0))
    return pl.pallas_call(
        functools.partial(_outproj_kernel, n_hg=n_hg),
        grid=(T // tm, n_hg + n_sb),
        in_specs=[pl.BlockSpec((tm, tk), lambda i, k: (i, jnp.minimum(k, n_hg - 1))),
                  pl.BlockSpec((tm, tk), lambda i, k: (i, jnp.maximum(k - n_hg, 0))),
                  pl.BlockSpec((tk, D), lambda i, k: (k, 0)),
                  row_blk, per_b, vec, vec, per_b, per_b,
                  pl.BlockSpec((E, D), lambda i, k: (0, 0))],
        out_specs=[row_blk, pl.BlockSpec((tm, D // 2), lambda i, k: (i, 0)),
                   pl.BlockSpec((E, tm), lambda i, k: (0, i))],
        out_shape=[jax.ShapeDtypeStruct((T, D), F32), jax.ShapeDtypeStruct((T, D // 2), U32),
                   jax.ShapeDtypeStruct((E, T), F32)],
        compiler_params=_params("arbitrary", "arbitrary"),
        name="out_proj",
    )(o_hg, o_sb, w_out_bf, x2, g1, post_w.reshape(1, D), ffn_w.reshape(1, D), sc2, sh2, wr_t)


def _topk_kernel(lg_ref, b_ref, idx_ref, gate_ref):
    l = lg_ref[...] + b_ref[...]
    E = l.shape[0]
    eid = lax.broadcasted_iota(jnp.int32, l.shape, 0)
    tops, ids = [], []
    for _ in range(TOP_K):
        m = jnp.max(l, axis=0, keepdims=True)
        sel = jnp.min(jnp.where(l == m, eid, E), axis=0, keepdims=True)
        tops.append(m)
        ids.append(sel)
        l = jnp.where(eid == sel, -jnp.inf, l)
    ex = [jnp.exp(t - tops[0]) for t in tops]
    den = ex[0]
    for t in ex[1:]:
        den = den + t
    idx_ref[...] = jnp.concatenate(ids, axis=0)
    gate_ref[...] = jnp.concatenate([t / den for t in ex], axis=0)


def _topk(logits_t, b_router):
    E, T = logits_t.shape
    tt = _pick(T, (2048, 1024, 512, 256, 128))
    return pl.pallas_call(
        _topk_kernel,
        grid=(T // tt,),
        in_specs=[pl.BlockSpec((E, tt), lambda i: (0, i)), pl.BlockSpec((E, 1), lambda i: (0, 0))],
        out_specs=[pl.BlockSpec((TOP_K, tt), lambda i: (0, i)), pl.BlockSpec((TOP_K, tt), lambda i: (0, i))],
        out_shape=[jax.ShapeDtypeStruct((TOP_K, T), jnp.int32), jax.ShapeDtypeStruct((TOP_K, T), F32)],
        compiler_params=_params("arbitrary"),
        name="topk",
    )(logits_t, b_router.reshape(E, 1))


def _moe_up_kernel(be_ref, nv_ref, tok_ref, tokn_ref, h_ref, wg_ref, wl_ref, bg_ref, bl_ref, o_ref,
                   stage, xb_ref, sem, *, rows, per):
    i = pl.program_id(0)
    c = pl.program_id(1)
    nv = nv_ref[0]
    valid = i < nv
    half = stage.shape[1]

    def row_copy(tok, r):
        return pltpu.make_async_copy(h_ref.at[pl.ds(tok, 1), :], stage.at[pl.ds(r, 1), :], sem)

    @pl.when((i == 0) & (c == 0))
    def _():
        def start(r, carry):
            row_copy(tok_ref[0, r], r).start()
            return carry
        lax.fori_loop(0, rows, start, 0)

    @pl.when((i <= nv) & (c == 0))
    def _():
        def wait(r, carry):
            row_copy(0, r).wait()
            return carry
        lax.fori_loop(0, rows, wait, 0, unroll=8)

    @pl.when(valid & (c == 0))
    def _():
        lo, hi = _unpack_bf16_pair(stage[...])
        xb_ref[:, :half] = lo
        xb_ref[:, half:] = hi

    @pl.when(valid)
    def _():
        xb = xb_ref[...]
        glu = _dot(xb, wg_ref[...].astype(BF16)) + bg_ref[...]
        lin = _dot(xb, wl_ref[...].astype(BF16)) + bl_ref[...]
        glu = jnp.minimum(glu, SWIGLU_LIMIT)
        lin = jnp.clip(lin, -SWIGLU_LIMIT, SWIGLU_LIMIT)
        o_ref[...] = (glu * jax.nn.sigmoid(SWIGLU_ALPHA * glu) * (lin + 1.0)).astype(BF16)
        for u in range(per):
            r = c * per + u
            row_copy(tokn_ref[0, r], r).start()

    @pl.when(jnp.logical_not(valid))
    def _():
        o_ref[...] = jnp.zeros_like(o_ref)


def _moe_up(h2p, row_tok, w_gu, b_gu, blk_expert, n_valid, rows):
    n_rows = row_tok.shape[0]
    half = h2p.shape[1]
    D = 2 * half
    E, _, two_ff = w_gu.shape
    d_ff = two_ff // 2
    tf = _pick(d_ff, (256, 128))
    nc = d_ff // tf
    nb = n_rows // rows
    assert rows % nc == 0
    tok3 = row_tok.reshape(nb, 1, rows)

    def chunk(i, c, nv):
        return jnp.where(i < nv[0], c, nc - 1)

    grid_spec = pltpu.PrefetchScalarGridSpec(
        num_scalar_prefetch=2,
        grid=(nb, nc),
        in_specs=[pl.BlockSpec((None, 1, rows), lambda i, c, be, nv: (i, 0, 0), memory_space=pltpu.SMEM),
                  pl.BlockSpec((None, 1, rows), lambda i, c, be, nv: (jnp.minimum(i + 1, nb - 1), 0, 0),
                               memory_space=pltpu.SMEM),
                  pl.BlockSpec(memory_space=pl.ANY),
                  pl.BlockSpec((None, D, tf), lambda i, c, be, nv: (be[i], 0, chunk(i, c, nv))),
                  pl.BlockSpec((None, D, tf), lambda i, c, be, nv: (be[i], 0, nc + chunk(i, c, nv))),
                  pl.BlockSpec((None, 1, tf), lambda i, c, be, nv: (be[i], 0, chunk(i, c, nv))),
                  pl.BlockSpec((None, 1, tf), lambda i, c, be, nv: (be[i], 0, nc + chunk(i, c, nv)))],
        out_specs=pl.BlockSpec((rows, tf), lambda i, c, be, nv: (i, c)),
        scratch_shapes=[pltpu.VMEM((rows, half), U32), pltpu.VMEM((rows, D), BF16), pltpu.SemaphoreType.DMA(())],
    )
    return pl.pallas_call(
        functools.partial(_moe_up_kernel, rows=rows, per=rows // nc),
        grid_spec=grid_spec,
        out_shape=jax.ShapeDtypeStruct((n_rows, d_ff), BF16),
        compiler_params=_params("arbitrary", "arbitrary"),
        name="moe_up",
    )(blk_expert, n_valid, tok3, tok3, h2p, w_gu, w_gu, b_gu.reshape(E, 1, two_ff), b_gu.reshape(E, 1, two_ff))


def _moe_down_kernel(be_ref, nv_ref, a_ref, w_ref, b_ref, o_ref):
    valid = pl.program_id(0) < nv_ref[0]

    @pl.when(valid)
    def _():
        y = _dot(a_ref[...], w_ref[...].astype(BF16)) + b_ref[...]
        hw = y.shape[1] // 2
        o_ref[...] = _pack_bf16_pair(y[:, :hw], y[:, hw:])

    @pl.when(jnp.logical_not(valid))
    def _():
        o_ref[...] = jnp.zeros_like(o_ref)


def _moe_down_chunk(D):
    return _pick(D, (2048, 1024, 512, 256))


def _moe_down(act, w_down, b_down, blk_expert, n_valid, rows):
    n_rows, d_ff = act.shape
    E, _, D = w_down.shape
    tn = _moe_down_chunk(D)
    nj = D // tn
    nb = n_rows // rows

    def col(i, j, nv):
        return jnp.where(i < nv[0], j, nj - 1)

    grid_spec = pltpu.PrefetchScalarGridSpec(
        num_scalar_prefetch=2,
        grid=(nb, nj),
        in_specs=[pl.BlockSpec((rows, d_ff), lambda i, j, be, nv: (jnp.minimum(i, nv[0] - 1), 0)),
                  pl.BlockSpec((None, d_ff, tn), lambda i, j, be, nv: (be[i], 0, col(i, j, nv))),
                  pl.BlockSpec((None, 1, tn), lambda i, j, be, nv: (be[i], 0, col(i, j, nv)))],
        out_specs=pl.BlockSpec((rows, tn // 2), lambda i, j, be, nv: (i, j)),
    )
    return pl.pallas_call(
        _moe_down_kernel,
        grid_spec=grid_spec,
        out_shape=jax.ShapeDtypeStruct((n_rows, D // 2), U32),
        compiler_params=_params("arbitrary", "arbitrary"),
        name="moe_down",
    )(blk_expert, n_valid, act, w_down, b_down.reshape(E, 1, D))


def _combine_kernel(pos_ref, posn_ref, gate_ref, y_ref, x1_ref, g2_ref, nw_ref, o_ref, buf, sem, *, tm, pair_w):
    i = pl.program_id(0)
    slot = i % 2

    def row_copy(src_row, s, k, r):
        return pltpu.make_async_copy(y_ref.at[pl.ds(src_row, 1), :], buf.at[s, k, pl.ds(r, 1), :], sem.at[s])

    def wait_block(s):
        for k in range(TOP_K):
            def wait(r, carry, k=k):
                row_copy(0, s, k, r).wait()
                return carry
            lax.fori_loop(0, tm, wait, 0, unroll=8)

    @pl.when(i == 0)
    def _():
        for k in range(TOP_K):
            def start(r, carry, k=k):
                row_copy(pos_ref[k, r], 0, k, r).start()
                return carry
            lax.fori_loop(0, tm, start, 0)

    wait_block(slot)

    grp = 16
    for g0 in range(0, tm, grp):
        for k in range(TOP_K):
            for r in range(g0, g0 + grp):
                row_copy(posn_ref[k, r], 1 - slot, k, r).start()
        rs = slice(g0, g0 + grp)
        m = None
        for k in range(TOP_K):
            words = buf[slot, k, rs, :]
            pieces = []
            for c0 in range(0, words.shape[1], pair_w):
                lo, hi = _unpack_bf16_pair(words[:, c0:c0 + pair_w])
                pieces += [lo.astype(F32), hi.astype(F32)]
            yk = gate_ref[k, rs, :] * jnp.concatenate(pieces, axis=1)
            m = yk if m is None else m + yk
        o_ref[rs, :] = x1_ref[rs, :] + g2_ref[...] * _rms(m, nw_ref[...])

    @pl.when(i == pl.num_programs(0) - 1)
    def _():
        wait_block(1 - slot)


def _moe_combine(y, pos, gates, x1, g2, norm_w, seq):
    T, D = x1.shape
    tm = _pick(seq, (128,))
    nt = T // tm
    pos_b = pos.reshape(TOP_K, nt, tm).transpose(1, 0, 2)
    return pl.pallas_call(
        functools.partial(_combine_kernel, tm=tm, pair_w=_moe_down_chunk(D) // 2),
        grid=(nt,),
        in_specs=[pl.BlockSpec((None, TOP_K, tm), lambda i: (i, 0, 0), memory_space=pltpu.SMEM),
                  pl.BlockSpec((None, TOP_K, tm), lambda i: (jnp.minimum(i + 1, nt - 1), 0, 0),
                               memory_space=pltpu.SMEM),
                  pl.BlockSpec((TOP_K, tm, 1), lambda i: (0, i, 0)),
                  pl.BlockSpec(memory_space=pl.ANY),
                  pl.BlockSpec((tm, D), lambda i: (i, 0)),
                  pl.BlockSpec((None, 1, D), lambda i: ((i * tm) // seq, 0, 0)),
                  pl.BlockSpec((1, D), lambda i: (0, 0))],
        out_specs=pl.BlockSpec((tm, D), lambda i: (i, 0)),
        out_shape=jax.ShapeDtypeStruct((T, D), F32),
        scratch_shapes=[pltpu.VMEM((2, TOP_K, tm, D // 2), U32), pltpu.SemaphoreType.DMA((2,))],
        compiler_params=_params("arbitrary"),
        name="moe_combine",
    )(pos_b, pos_b, gates.reshape(TOP_K, T, 1), y, x1, g2, norm_w.reshape(1, D))


def _routing_tables(top_idx, n_experts, rows):
    T = top_idx.shape[1]
    n_assign = T * TOP_K
    i32 = jnp.int32
    flat_e = top_idx.T.reshape(-1)
    iota = jnp.arange(n_assign, dtype=i32)
    _, order = lax.sort((flat_e, iota), num_keys=1)
    _, inv = lax.sort((order, iota), num_keys=1)
    eids = jnp.arange(n_experts, dtype=i32)
    counts = jnp.sum((flat_e[None, :] == eids[:, None]).astype(i32), axis=1)
    padded = (counts + rows - 1) // rows * rows
    pad_end = jnp.cumsum(padded)
    pad_start = pad_end - padded
    start = jnp.cumsum(counts) - counts
    n_rows = -(-n_assign // rows) * rows + n_experts * rows
    n_blocks = n_rows // rows
    pos = (pad_start[flat_e] + inv - start[flat_e]).astype(i32).reshape(T, TOP_K).T

    def expert_of(r):
        return jnp.minimum(jnp.sum((pad_end[None, :] <= r[:, None]).astype(i32), axis=1), n_experts - 1)

    r = jnp.arange(n_rows, dtype=i32)
    row_e = expert_of(r)
    off = r - pad_start[row_e]
    real = (off < counts[row_e]) & (r < pad_end[-1])
    src = jnp.clip(start[row_e] + off, 0, n_assign - 1)
    row_tok = jnp.where(real, order[src] // TOP_K, 0).astype(i32)
    n_valid = (pad_end[-1] // rows).astype(i32)
    blk = jnp.minimum(jnp.arange(n_blocks, dtype=i32), n_valid - 1) * rows
    return row_tok, pos, expert_of(blk).astype(i32), n_valid.reshape(1)


def kernel(x, c, w_mod, b_mod, mix_pre_norm, mix_post_norm, w_in, hg_lb, hg_norm_w, w_out, ffn_pre_norm,
           ffn_post_norm, w_router, b_router, w_gu, b_gu, w_down, b_down):
    B, S, D = x.shape
    depth = w_mod.shape[0]
    d_hg = hg_lb.shape[1]
    d_sb = w_out.shape[1] - d_hg
    E = w_router.shape[2]
    T = B * S
    assert depth == 1 and S % SB_BLOCK == 0 and d_hg % HEAD_DIM == 0 and d_sb % HEAD_DIM == 0
    moe_rows = MOE_ROWS

    x2 = x.reshape(T, D)
    for l in range(depth):
        mod = _modulation(c, w_mod[l], b_mod[l])
        sh1, sc1, g1, sh2, sc2, g2 = [mod[:, j * D:(j + 1) * D].reshape(B, 1, D) for j in range(6)]

        proj = _in_proj(x2, mix_pre_norm[l], sc1, sh1, w_in[l].astype(BF16), S)
        o_hg = _hgrn2(proj, hg_lb, hg_norm_w[l], B, S, d_hg)
        o_sb = _stick_breaking(proj, B, S, d_hg, d_sb)
        x1, h2p, logits_t = _out_proj(o_hg, o_sb, w_out[l].astype(BF16), x2, g1, mix_post_norm[l],
                                      ffn_pre_norm[l], sc2, sh2, w_router[l].T.astype(BF16), S)

        top_idx, gates = _topk(logits_t, b_router[l])
        row_tok, pos, blk_expert, n_valid = _routing_tables(top_idx, E, moe_rows)
        act = _moe_up(h2p, row_tok, w_gu[l], b_gu[l], blk_expert, n_valid, moe_rows)
        y = _moe_down(act, w_down[l], b_down[l], blk_expert, n_valid, moe_rows)
        x2 = _moe_combine(y, pos, gates, x1, g2, ffn_post_norm[l], S)
    return x2.reshape(B, S, D)
```

```python
import functools

import jax
import jax.numpy as jnp
from jax import lax
from jax.experimental import pallas as pl
from jax.experimental.pallas import tpu as pltpu

HEAD_DIM = 128
HG_CHUNK = 64
HG_SUB = 16
SB_BLOCK = 128
TOP_K = 4
MOE_ROWS = 768
SWIGLU_ALPHA = 1.702
SWIGLU_LIMIT = 7.0
NORM_EPS = 1e-6
VMEM_LIMIT_BYTES = 56 * 1024 * 1024
VMEM_LIMIT_EXPERT_RESIDENT_BYTES = 60 * 1024 * 1024

F32 = jnp.float32
BF16 = jnp.bfloat16
U32 = jnp.uint32


def _pick(dim, prefs):
    for p in prefs:
        if dim % p == 0:
            return p
    return dim


def _params(*sem):
    return pltpu.CompilerParams(dimension_semantics=sem, vmem_limit_bytes=VMEM_LIMIT_BYTES)


def _rms(t, w):
    return t * lax.rsqrt(jnp.mean(t * t, axis=-1, keepdims=True) + NORM_EPS) * w


def _dot(a, b):
    return jnp.dot(a, b, preferred_element_type=F32)


def _dot_nt(a, b):
    return lax.dot_general(a, b, (((1,), (1,)), ((), ())), preferred_element_type=F32)


def _pack_bf16_pair(lo, hi):
    lo_bits = lax.bitcast_convert_type(lo.astype(BF16).astype(F32), U32) >> 16
    hi_bits = lax.bitcast_convert_type(hi.astype(BF16).astype(F32), U32) & jnp.uint32(0xFFFF0000)
    return hi_bits | lo_bits


def _unpack_bf16_pair(words):
    lo = lax.bitcast_convert_type(words << 16, F32).astype(BF16)
    hi = lax.bitcast_convert_type(words & jnp.uint32(0xFFFF0000), F32).astype(BF16)
    return lo, hi


def _mod_kernel(c_ref, w_ref, b_ref, o_ref):
    c = c_ref[...]
    a = (c * jax.nn.sigmoid(c)).astype(BF16)
    o_ref[...] = _dot(a, w_ref[...].astype(BF16)) + b_ref[...]


def _modulation(c, w_mod, b_mod):
    B, D = c.shape
    N = w_mod.shape[1]
    tn = _pick(N, (512, 256, 128))
    return pl.pallas_call(
        _mod_kernel,
        grid=(N // tn,),
        in_specs=[pl.BlockSpec((B, D), lambda j: (0, 0)),
                  pl.BlockSpec((D, tn), lambda j: (0, j)),
                  pl.BlockSpec((1, tn), lambda j: (0, j))],
        out_specs=pl.BlockSpec((B, tn), lambda j: (0, j)),
        out_shape=jax.ShapeDtypeStruct((B, N), F32),
        compiler_params=_params("arbitrary"),
        name="modulation",
    )(c, w_mod, b_mod.reshape(1, N))


def _inproj_kernel(x_ref, nw_ref, sc_ref, sh_ref, w_ref, o_ref, h_ref):
    @pl.when(pl.program_id(1) == 0)
    def _():
        h = _rms(x_ref[...], nw_ref[...]) * (1.0 + sc_ref[...]) + sh_ref[...]
        h_ref[...] = h.astype(BF16)

    o_ref[...] = _dot(h_ref[...], w_ref[...])


def _in_proj(x2, norm_w, sc, sh, w_bf, seq):
    T, D = x2.shape
    N = w_bf.shape[1]
    tm = _pick(seq, (512, 256, 128))
    tn = _pick(N, (1024, 512, 256, 128))
    return pl.pallas_call(
        _inproj_kernel,
        grid=(T // tm, N // tn),
        in_specs=[pl.BlockSpec((tm, D), lambda i, j: (i, 0)),
                  pl.BlockSpec((1, D), lambda i, j: (0, 0)),
                  pl.BlockSpec((None, 1, D), lambda i, j: ((i * tm) // seq, 0, 0)),
                  pl.BlockSpec((None, 1, D), lambda i, j: ((i * tm) // seq, 0, 0)),
                  pl.BlockSpec((D, tn), lambda i, j: (0, j))],
        out_specs=pl.BlockSpec((tm, tn), lambda i, j: (i, j)),
        out_shape=jax.ShapeDtypeStruct((T, N), F32),
        scratch_shapes=[pltpu.VMEM((tm, D), BF16)],
        compiler_params=_params("arbitrary", "arbitrary"),
        name="in_proj",
    )(x2, norm_w.reshape(1, D), sc, sh, w_bf)


def _split3(t):
    hi = t.astype(BF16)
    r1 = t - hi.astype(F32)
    mid = r1.astype(BF16)
    lo = (r1 - mid.astype(F32)).astype(BF16)
    return hi, mid, lo


def _hgrn_kernel(lb_ref, nw_ref, q_ref, f_ref, v_ref, g_ref, o_ref, st_ref, b_s, kk_s, v_s, od_s, *, n_chunks, heads):
    C, SUB = HG_CHUNK, HG_SUB

    @pl.when(pl.program_id(2) == 0)
    def _():
        st_ref[...] = jnp.zeros_like(st_ref)

    hg = lb_ref[...]
    e = jnp.exp(hg - jnp.max(hg, axis=0, keepdims=True))
    lb_all = e[0:1, :] / jnp.sum(e, axis=0, keepdims=True)
    nw = nw_ref[...]
    row = lax.broadcasted_iota(jnp.int32, (C, C), 0)
    col = lax.broadcasted_iota(jnp.int32, (C, C), 1)
    tri = jnp.where(row >= col, 1.0, 0.0).astype(BF16)
    rows = lax.broadcasted_iota(jnp.int32, (C, 1), 0)
    rows8 = lax.broadcasted_iota(jnp.int32, (8, 1), 0)

    def lanes(h):
        return slice(h * HEAD_DIM, (h + 1) * HEAD_DIM)

    def decay(r, h):
        lb = lb_all[:, lanes(h)]
        q = q_ref[pl.ds(r, C), lanes(h)]
        v = v_ref[pl.ds(r, C), lanes(h)]
        f = lb + (1.0 - lb) * jax.nn.sigmoid(f_ref[pl.ds(r, C), lanes(h)])
        kk = 1.0 - f
        hi, mid, lo = _split3(jnp.log(f))
        b = _dot(tri, hi) + _dot(tri, mid) + _dot(tri, lo)
        b_s[h] = b
        kk_s[h] = kk
        v_s[h] = v
        return q, v, kk, b

    def matmul_terms(h, q, v, kk, b):
        b_last = b_s[h, C - 1:C, :]
        st = st_ref[h]
        o = _dot_nt((q * jnp.exp(b)).astype(BF16), st.astype(BF16))
        p = jnp.zeros((C, C), F32)
        for i in range(1, C // SUB):
            bi = b_s[h, i * SUB - 1:i * SUB, :]
            in_i = (rows >= i * SUB) & (rows < (i + 1) * SUB)
            qh = jnp.where(in_i, q * jnp.exp(jnp.minimum(b - bi, 0.0)), 0.0)
            kh = jnp.where(rows < i * SUB, kk * jnp.exp(jnp.minimum(bi - b, 0.0)), 0.0)
            p = p + _dot_nt(qh.astype(BF16), kh.astype(BF16))
        kt = (kk * jnp.exp(b_last - b)).astype(BF16)
        st_ref[h] = st * jnp.exp(b_last) + _dot(v.T.astype(BF16), kt)
        return o, p

    def pairwise(h, q, b):
        for i in range(C // SUB):
            for g0 in range(0, SUB, 8):
                r0 = i * SUB + g0
                qi = q[r0:r0 + 8, :]
                bi = b[r0:r0 + 8, :]
                acc = jnp.zeros((8, HEAD_DIM), F32)
                for s in range(g0 + 8):
                    rs = i * SUB + s
                    a = qi * jnp.exp(bi - b_s[h, rs:rs + 1, :]) * kk_s[h, rs:rs + 1, :]
                    sc = jnp.sum(a, axis=-1, keepdims=True)
                    if s >= g0:
                        sc = jnp.where(rows8 >= s - g0, sc, 0.0)
                    acc = acc + sc * v_s[h, rs:rs + 1, :]
                od_s[h, r0:r0 + 8, :] = acc

    def chunk(c, carry):
        r = pl.multiple_of(c * C, C)
        hs = range(heads)
        qvkb = [decay(r, h) for h in hs]
        op = [matmul_terms(h, *qvkb[h]) for h in hs]
        for h in hs:
            pairwise(h, qvkb[h][0], qvkb[h][3])
        for h in hs:
            o, p = op[h]
            o = o + _dot(p.astype(BF16), qvkb[h][1].astype(BF16)) + od_s[h]
            g = g_ref[pl.ds(r, C), lanes(h)]
            o_ref[pl.ds(r, C), lanes(h)] = (_rms(o, nw) * (g * jax.nn.sigmoid(g))).astype(BF16)
        return carry

    lax.fori_loop(0, n_chunks, chunk, 0)


def _hgrn2(proj, hg_lb, hg_norm_w, batch, seq, d_hg):
    T = proj.shape[0]
    H = d_hg // HEAD_DIM
    G = _pick(H, (4, 2, 1))
    W = G * HEAD_DIM
    L = _pick(seq, (512, 256, 128, 64))
    nl = seq // L
    n_slots = hg_lb.shape[0]

    def col(off):
        return pl.BlockSpec((L, W), lambda b, h, s: (b * nl + s, off * (H // G) + h))

    return pl.pallas_call(
        functools.partial(_hgrn_kernel, n_chunks=L // HG_CHUNK, heads=G),
        grid=(batch, H // G, nl),
        in_specs=[pl.BlockSpec((n_slots, W), lambda b, h, s: (0, h)),
                  pl.BlockSpec((1, HEAD_DIM), lambda b, h, s: (0, 0)),
                  col(0), col(1), col(2), col(3)],
        out_specs=pl.BlockSpec((L, W), lambda b, h, s: (b * nl + s, h)),
        out_shape=jax.ShapeDtypeStruct((T, d_hg), BF16),
        scratch_shapes=[pltpu.VMEM((G, HEAD_DIM, HEAD_DIM), F32)] + [pltpu.VMEM((G, HG_CHUNK, HEAD_DIM), F32)] * 4,
        compiler_params=_params("arbitrary", "arbitrary", "arbitrary"),
        name="hgrn2",
    )(hg_lb, hg_norm_w.reshape(1, HEAD_DIM), proj, proj, proj, proj)


def _sb_kernel(q_ref, k_ref, v_ref, o_ref, acc_ref, tail_ref, *, heads, tq):
    TK = SB_BLOCK
    i = pl.program_id(2)
    n_kb = (i + 1) * (tq // TK)
    scale = HEAD_DIM ** -0.5
    row = lax.broadcasted_iota(jnp.int32, (tq, TK), 0)
    col = lax.broadcasted_iota(jnp.int32, (tq, TK), 1)
    r2 = lax.broadcasted_iota(jnp.int32, (2 * TK, 2 * TK), 0)
    c2 = lax.broadcasted_iota(jnp.int32, (2 * TK, 2 * TK), 1)
    r2 = jnp.where(r2 >= TK, r2 - TK, r2)
    op = jnp.where((c2 >= TK) | (r2 >= c2), 1.0, 0.0).astype(BF16)
    acc_ref[...] = jnp.zeros_like(acc_ref)
    tail_ref[...] = jnp.zeros_like(tail_ref)

    def body(jj, carry, *, masked):
        j = n_kb - 1 - jj
        r = pl.multiple_of(j * TK, TK)
        strict = (col + j * TK) < (row + i * tq)
        hs = range(heads)
        ls = [slice(h * HEAD_DIM, (h + 1) * HEAD_DIM) for h in hs]
        zs = [_dot_nt(q_ref[:, ls[h]].astype(BF16), k_ref[pl.ds(r, TK), ls[h]].astype(BF16)) * scale for h in hs]
        sums = []
        for z in zs:
            lk = -(jnp.maximum(z, 0.0) + jnp.log(1.0 + jnp.exp(-jnp.abs(z))))
            if masked:
                lk = jnp.where(strict, lk, 0.0)
            hi = lk.astype(BF16)
            lo = (lk - hi.astype(F32)).astype(BF16)
            sums.append(_dot(jnp.concatenate([hi, lo], axis=1), op))
        for h in hs:
            log_w = zs[h] + sums[h][:, :TK] + tail_ref[h]
            if masked:
                log_w = jnp.where(strict, log_w, -jnp.inf)
            acc_ref[h] += _dot(jnp.exp(log_w).astype(BF16), v_ref[pl.ds(r, TK), ls[h]].astype(BF16))
            tail_ref[h] += sums[h][:, TK:]
        return carry

    n_diag = tq // TK
    lax.fori_loop(0, n_diag, functools.partial(body, masked=True), 0)
    lax.fori_loop(n_diag, n_kb, functools.partial(body, masked=False), 0)
    for h in range(heads):
        o_ref[:, h * HEAD_DIM:(h + 1) * HEAD_DIM] = acc_ref[h].astype(BF16)


def _stick_breaking(proj, batch, seq, d_hg, d_sb):
    T = proj.shape[0]
    H = d_sb // HEAD_DIM
    G = _pick(H, (4, 2, 1))
    W = G * HEAD_DIM
    assert (4 * d_hg) % W == 0
    tq = _pick(seq, (256, 128))
    nq = seq // tq
    base = 4 * d_hg // W
    return pl.pallas_call(
        functools.partial(_sb_kernel, heads=G, tq=tq),
        grid=(batch, H // G, nq),
        in_specs=[pl.BlockSpec((tq, W), lambda b, h, i: (b * nq + i, base + h)),
                  pl.BlockSpec((seq, W), lambda b, h, i: (b, base + H // G + h)),
                  pl.BlockSpec((seq, W), lambda b, h, i: (b, base + 2 * (H // G) + h))],
        out_specs=pl.BlockSpec((tq, W), lambda b, h, i: (b * nq + i, h)),
        out_shape=jax.ShapeDtypeStruct((T, d_sb), BF16),
        scratch_shapes=[pltpu.VMEM((G, tq, HEAD_DIM), F32), pltpu.VMEM((G, tq, HEAD_DIM), F32)],
        compiler_params=_params("arbitrary", "arbitrary", "arbitrary"),
        name="stickbreak",
    )(proj, proj, proj)


def _outproj_kernel(ohg_ref, osb_ref, w_ref, x_ref, g1_ref, pw_ref, fw_ref, sc_ref, sh_ref, wr_ref,
                    x1_ref, h2p_ref, lg_ref, *, n_hg):
    k = pl.program_id(1)

    @pl.when(k == 0)
    def _():
        x1_ref[...] = jnp.zeros_like(x1_ref)

    @pl.when(k < n_hg)
    def _():
        x1_ref[...] += _dot(ohg_ref[...], w_ref[...])

    @pl.when(k >= n_hg)
    def _():
        x1_ref[...] += _dot(osb_ref[...], w_ref[...])

    @pl.when(k == pl.num_programs(1) - 1)
    def _():
        x1 = x_ref[...] + g1_ref[...] * _rms(x1_ref[...], pw_ref[...])
        x1_ref[...] = x1
        h2 = _rms(x1, fw_ref[...]) * (1.0 + sc_ref[...]) + sh_ref[...]
        half = h2.shape[1] // 2
        h2p_ref[...] = _pack_bf16_pair(h2[:, :half], h2[:, half:])
        lg_ref[...] = _dot_nt(wr_ref[...], h2.astype(BF16))


def _out_proj(o_hg, o_sb, w_out_bf, x2, g1, post_w, ffn_w, sc2, sh2, wr_t, seq):
    T, D = x2.shape
    d_hg, d_sb = o_hg.shape[1], o_sb.shape[1]
    E = wr_t.shape[0]
    tm = _pick(seq, (512, 256, 128))
    tk = _pick(d_hg, (512, 256, 128))
    assert d_sb % tk == 0 and (D // 2) % HEAD_DIM == 0
    n_hg, n_sb = d_hg // tk, d_sb // tk
    vec = pl.BlockSpec((1, D), lambda i, k: (0, 0))
    per_b = pl.BlockSpec((None, 1, D), lambda i, k: ((i * tm) // seq, 0, 0))
    row_blk = pl.BlockSpec((tm, D), lambda i, k: (i, 0))
    return pl.pallas_call(
        functools.partial(_outproj_kernel, n_hg=n_hg),
        grid=(T // tm, n_hg + n_sb),
        in_specs=[pl.BlockSpec((tm, tk), lambda i, k: (i, jnp.minimum(k, n_hg - 1))),
                  pl.BlockSpec((tm, tk), lambda i, k: (i, jnp.maximum(k - n_hg, 0))),
                  pl.BlockSpec((tk, D), lambda i, k: (k, 0)),
                  row_blk, per_b, vec, vec, per_b, per_b,
                  pl.BlockSpec((E, D), lambda i, k: (0, 0))],
        out_specs=[row_blk, pl.BlockSpec((tm, D // 2), lambda i, k: (i, 0)),
                   pl.BlockSpec((E, tm), lambda i, k: (0, i))],
        out_shape=[jax.ShapeDtypeStruct((T, D), F32), jax.ShapeDtypeStruct((T, D // 2), U32),
                   jax.ShapeDtypeStruct((E, T), F32)],
        compiler_params=_params("arbitrary", "arbitrary"),
        name="out_proj",
    )(o_hg, o_sb, w_out_bf, x2, g1, post_w.reshape(1, D), ffn_w.reshape(1, D), sc2, sh2, wr_t)


def _topk_kernel(lg_ref, b_ref, idx_ref, gate_ref):
    l = lg_ref[...] + b_ref[...]
    E = l.shape[0]
    eid = lax.broadcasted_iota(jnp.int32, l.shape, 0)
    tops, ids = [], []
    for _ in range(TOP_K):
        m = jnp.max(l, axis=0, keepdims=True)
        sel = jnp.min(jnp.where(l == m, eid, E), axis=0, keepdims=True)
        tops.append(m)
        ids.append(sel)
        l = jnp.where(eid == sel, -jnp.inf, l)
    ex = [jnp.exp(t - tops[0]) for t in tops]
    den = ex[0]
    for t in ex[1:]:
        den = den + t
    idx_ref[...] = jnp.concatenate(ids, axis=0)
    gate_ref[...] = jnp.concatenate([t / den for t in ex], axis=0)


def _topk(logits_t, b_router):
    E, T = logits_t.shape
    tt = _pick(T, (2048, 1024, 512, 256, 128))
    return pl.pallas_call(
        _topk_kernel,
        grid=(T // tt,),
        in_specs=[pl.BlockSpec((E, tt), lambda i: (0, i)), pl.BlockSpec((E, 1), lambda i: (0, 0))],
        out_specs=[pl.BlockSpec((TOP_K, tt), lambda i: (0, i)), pl.BlockSpec((TOP_K, tt), lambda i: (0, i))],
        out_shape=[jax.ShapeDtypeStruct((TOP_K, T), jnp.int32), jax.ShapeDtypeStruct((TOP_K, T), F32)],
        compiler_params=_params("arbitrary"),
        name="topk",
    )(logits_t, b_router.reshape(E, 1))


def _moe_up_kernel(be_ref, nv_ref, first_ref, tok_ref, tokn_ref, h_ref, wg_ref, wl_ref, bg_ref, bl_ref, o_ref,
                   stage, xb_ref, wres_ref, sem, *, rows, per):
    i = pl.program_id(0)
    c = pl.program_id(1)
    nv = nv_ref[0]
    valid = i < nv
    half = stage.shape[1]

    @pl.when(valid & (first_ref[i] == 1))
    def _():
        wres_ref[0, c] = wg_ref[...].astype(BF16)
        wres_ref[1, c] = wl_ref[...].astype(BF16)

    def row_copy(tok, r):
        return pltpu.make_async_copy(h_ref.at[pl.ds(tok, 1), :], stage.at[pl.ds(r, 1), :], sem)

    @pl.when((i == 0) & (c == 0))
    def _():
        def start(r, carry):
            row_copy(tok_ref[0, r], r).start()
            return carry
        lax.fori_loop(0, rows, start, 0)

    @pl.when((i <= nv) & (c == 0))
    def _():
        def wait(r, carry):
            row_copy(0, r).wait()
            return carry
        lax.fori_loop(0, rows, wait, 0, unroll=8)

    @pl.when(valid & (c == 0))
    def _():
        lo, hi = _unpack_bf16_pair(stage[...])
        xb_ref[:, :half] = lo
        xb_ref[:, half:] = hi

    @pl.when(valid)
    def _():
        xb = xb_ref[...]
        glu = _dot(xb, wres_ref[0, c]) + bg_ref[...]
        lin = _dot(xb, wres_ref[1, c]) + bl_ref[...]
        glu = jnp.minimum(glu, SWIGLU_LIMIT)
        lin = jnp.clip(lin, -SWIGLU_LIMIT, SWIGLU_LIMIT)
        o_ref[...] = (glu * jax.nn.sigmoid(SWIGLU_ALPHA * glu) * (lin + 1.0)).astype(BF16)
        for u in range(per):
            r = c * per + u
            row_copy(tokn_ref[0, r], r).start()

    @pl.when(jnp.logical_not(valid))
    def _():
        o_ref[...] = jnp.zeros_like(o_ref)


def _moe_up(h2p, row_tok, w_gu, b_gu, blk_expert, n_valid, rows):
    n_rows = row_tok.shape[0]
    half = h2p.shape[1]
    D = 2 * half
    E, _, two_ff = w_gu.shape
    d_ff = two_ff // 2
    tf = _pick(d_ff, (256, 128))
    nc = d_ff // tf
    nb = n_rows // rows
    assert rows % nc == 0
    tok3 = row_tok.reshape(nb, 1, rows)
    blocks = jnp.arange(nb, dtype=jnp.int32)
    first = ((blocks < n_valid[0]) & ((blocks == 0) | (blk_expert != jnp.roll(blk_expert, 1)))).astype(jnp.int32)

    def chunk(i, c, nv):
        return jnp.where(i < nv[0], c, nc - 1)

    def wchunk(i, c, fi):
        return jnp.where(fi[i] == 1, c, nc - 1)

    grid_spec = pltpu.PrefetchScalarGridSpec(
        num_scalar_prefetch=3,
        grid=(nb, nc),
        in_specs=[pl.BlockSpec((None, 1, rows), lambda i, c, be, nv, fi: (i, 0, 0), memory_space=pltpu.SMEM),
                  pl.BlockSpec((None, 1, rows), lambda i, c, be, nv, fi: (jnp.minimum(i + 1, nb - 1), 0, 0),
                               memory_space=pltpu.SMEM),
                  pl.BlockSpec(memory_space=pl.ANY),
                  pl.BlockSpec((None, D, tf), lambda i, c, be, nv, fi: (be[i], 0, wchunk(i, c, fi))),
                  pl.BlockSpec((None, D, tf), lambda i, c, be, nv, fi: (be[i], 0, nc + wchunk(i, c, fi))),
                  pl.BlockSpec((None, 1, tf), lambda i, c, be, nv, fi: (be[i], 0, chunk(i, c, nv))),
                  pl.BlockSpec((None, 1, tf), lambda i, c, be, nv, fi: (be[i], 0, nc + chunk(i, c, nv)))],
        out_specs=pl.BlockSpec((rows, tf), lambda i, c, be, nv, fi: (i, c)),
        scratch_shapes=[pltpu.VMEM((rows, half), U32), pltpu.VMEM((rows, D), BF16),
                        pltpu.VMEM((2, nc, D, tf), BF16), pltpu.SemaphoreType.DMA(())],
    )
    return pl.pallas_call(
        functools.partial(_moe_up_kernel, rows=rows, per=rows // nc),
        grid_spec=grid_spec,
        out_shape=jax.ShapeDtypeStruct((n_rows, d_ff), BF16),
        compiler_params=pltpu.CompilerParams(dimension_semantics=("arbitrary", "arbitrary"),
                                             vmem_limit_bytes=VMEM_LIMIT_EXPERT_RESIDENT_BYTES),
        name="moe_up",
    )(blk_expert, n_valid, first, tok3, tok3, h2p, w_gu, w_gu,
      b_gu.reshape(E, 1, two_ff), b_gu.reshape(E, 1, two_ff))


def _moe_down_kernel(be_ref, nv_ref, a_ref, w_ref, b_ref, o_ref):
    valid = pl.program_id(0) < nv_ref[0]

    @pl.when(valid)
    def _():
        y = _dot(a_ref[...], w_ref[...].astype(BF16)) + b_ref[...]
        hw = y.shape[1] // 2
        o_ref[...] = _pack_bf16_pair(y[:, :hw], y[:, hw:])

    @pl.when(jnp.logical_not(valid))
    def _():
        o_ref[...] = jnp.zeros_like(o_ref)


def _moe_down_chunk(D):
    return _pick(D, (2048, 1024, 512, 256))


def _moe_down(act, w_down, b_down, blk_expert, n_valid, rows):
    n_rows, d_ff = act.shape
    E, _, D = w_down.shape
    tn = _moe_down_chunk(D)
    nj = D // tn
    nb = n_rows // rows

    def col(i, j, nv):
        return jnp.where(i < nv[0], j, nj - 1)

    grid_spec = pltpu.PrefetchScalarGridSpec(
        num_scalar_prefetch=2,
        grid=(nb, nj),
        in_specs=[pl.BlockSpec((rows, d_ff), lambda i, j, be, nv: (jnp.minimum(i, nv[0] - 1), 0)),
                  pl.BlockSpec((None, d_ff, tn), lambda i, j, be, nv: (be[i], 0, col(i, j, nv))),
                  pl.BlockSpec((None, 1, tn), lambda i, j, be, nv: (be[i], 0, col(i, j, nv)))],
        out_specs=pl.BlockSpec((rows, tn // 2), lambda i, j, be, nv: (i, j)),
    )
    return pl.pallas_call(
        _moe_down_kernel,
        grid_spec=grid_spec,
        out_shape=jax.ShapeDtypeStruct((n_rows, D // 2), U32),
        compiler_params=_params("arbitrary", "arbitrary"),
        name="moe_down",
    )(blk_expert, n_valid, act, w_down, b_down.reshape(E, 1, D))


def _combine_kernel(pos_ref, posn_ref, gate_ref, y_ref, x1_ref, g2_ref, nw_ref, o_ref, buf, sem, *, tm, pair_w):
    i = pl.program_id(0)
    slot = i % 2

    def row_copy(src_row, s, k, r):
        return pltpu.make_async_copy(y_ref.at[pl.ds(src_row, 1), :], buf.at[s, k, pl.ds(r, 1), :], sem.at[s])

    def wait_block(s):
        for k in range(TOP_K):
            def wait(r, carry, k=k):
                row_copy(0, s, k, r).wait()
                return carry
            lax.fori_loop(0, tm, wait, 0, unroll=8)

    @pl.when(i == 0)
    def _():
        for k in range(TOP_K):
            def start(r, carry, k=k):
                row_copy(pos_ref[k, r], 0, k, r).start()
                return carry
            lax.fori_loop(0, tm, start, 0)

    wait_block(slot)

    grp = 16
    for g0 in range(0, tm, grp):
        for k in range(TOP_K):
            for r in range(g0, g0 + grp):
                row_copy(posn_ref[k, r], 1 - slot, k, r).start()
        rs = slice(g0, g0 + grp)
        m = None
        for k in range(TOP_K):
            words = buf[slot, k, rs, :]
            pieces = []
            for c0 in range(0, words.shape[1], pair_w):
                lo, hi = _unpack_bf16_pair(words[:, c0:c0 + pair_w])
                pieces += [lo.astype(F32), hi.astype(F32)]
            yk = gate_ref[k, rs, :] * jnp.concatenate(pieces, axis=1)
            m = yk if m is None else m + yk
        o_ref[rs, :] = x1_ref[rs, :] + g2_ref[...] * _rms(m, nw_ref[...])

    @pl.when(i == pl.num_programs(0) - 1)
    def _():
        wait_block(1 - slot)


def _moe_combine(y, pos, gates, x1, g2, norm_w, seq):
    T, D = x1.shape
    tm = _pick(seq, (128,))
    nt = T // tm
    pos_b = pos.reshape(TOP_K, nt, tm).transpose(1, 0, 2)
    return pl.pallas_call(
        functools.partial(_combine_kernel, tm=tm, pair_w=_moe_down_chunk(D) // 2),
        grid=(nt,),
        in_specs=[pl.BlockSpec((None, TOP_K, tm), lambda i: (i, 0, 0), memory_space=pltpu.SMEM),
                  pl.BlockSpec((None, TOP_K, tm), lambda i: (jnp.minimum(i + 1, nt - 1), 0, 0),
                               memory_space=pltpu.SMEM),
                  pl.BlockSpec((TOP_K, tm, 1), lambda i: (0, i, 0)),
                  pl.BlockSpec(memory_space=pl.ANY),
                  pl.BlockSpec((tm, D), lambda i: (i, 0)),
                  pl.BlockSpec((None, 1, D), lambda i: ((i * tm) // seq, 0, 0)),
                  pl.BlockSpec((1, D), lambda i: (0, 0))],
        out_specs=pl.BlockSpec((tm, D), lambda i: (i, 0)),
        out_shape=jax.ShapeDtypeStruct((T, D), F32),
        scratch_shapes=[pltpu.VMEM((2, TOP_K, tm, D // 2), U32), pltpu.SemaphoreType.DMA((2,))],
        compiler_params=_params("arbitrary"),
        name="moe_combine",
    )(pos_b, pos_b, gates.reshape(TOP_K, T, 1), y, x1, g2, norm_w.reshape(1, D))


def _routing_tables(top_idx, n_experts, rows):
    T = top_idx.shape[1]
    n_assign = T * TOP_K
    i32 = jnp.int32
    flat_e = top_idx.T.reshape(-1)
    iota = jnp.arange(n_assign, dtype=i32)
    _, order = lax.sort((flat_e, iota), num_keys=1)
    _, inv = lax.sort((order, iota), num_keys=1)
    eids = jnp.arange(n_experts, dtype=i32)
    counts = jnp.sum((flat_e[None, :] == eids[:, None]).astype(i32), axis=1)
    padded = (counts + rows - 1) // rows * rows
    pad_end = jnp.cumsum(padded)
    pad_start = pad_end - padded
    start = jnp.cumsum(counts) - counts
    n_rows = -(-n_assign // rows) * rows + n_experts * rows
    n_blocks = n_rows // rows
    pos = (pad_start[flat_e] + inv - start[flat_e]).astype(i32).reshape(T, TOP_K).T

    def expert_of(r):
        return jnp.minimum(jnp.sum((pad_end[None, :] <= r[:, None]).astype(i32), axis=1), n_experts - 1)

    r = jnp.arange(n_rows, dtype=i32)
    row_e = expert_of(r)
    off = r - pad_start[row_e]
    real = (off < counts[row_e]) & (r < pad_end[-1])
    src = jnp.clip(start[row_e] + off, 0, n_assign - 1)
    row_tok = jnp.where(real, order[src] // TOP_K, 0).astype(i32)
    n_valid = (pad_end[-1] // rows).astype(i32)
    blk = jnp.minimum(jnp.arange(n_blocks, dtype=i32), n_valid - 1) * rows
    return row_tok, pos, expert_of(blk).astype(i32), n_valid.reshape(1)


def kernel(x, c, w_mod, b_mod, mix_pre_norm, mix_post_norm, w_in, hg_lb, hg_norm_w, w_out, ffn_pre_norm,
           ffn_post_norm, w_router, b_router, w_gu, b_gu, w_down, b_down):
    B, S, D = x.shape
    depth = w_mod.shape[0]
    d_hg = hg_lb.shape[1]
    d_sb = w_out.shape[1] - d_hg
    E = w_router.shape[2]
    T = B * S
    assert depth == 1 and S % SB_BLOCK == 0 and d_hg % HEAD_DIM == 0 and d_sb % HEAD_DIM == 0
    moe_rows = MOE_ROWS

    x2 = x.reshape(T, D)
    for l in range(depth):
        mod = _modulation(c, w_mod[l], b_mod[l])
        sh1, sc1, g1, sh2, sc2, g2 = [mod[:, j * D:(j + 1) * D].reshape(B, 1, D) for j in range(6)]

        proj = _in_proj(x2, mix_pre_norm[l], sc1, sh1, w_in[l].astype(BF16), S)
        o_hg = _hgrn2(proj, hg_lb, hg_norm_w[l], B, S, d_hg)
        o_sb = _stick_breaking(proj, B, S, d_hg, d_sb)
        x1, h2p, logits_t = _out_proj(o_hg, o_sb, w_out[l].astype(BF16), x2, g1, mix_post_norm[l],
                                      ffn_pre_norm[l], sc2, sh2, w_router[l].T.astype(BF16), S)

        top_idx, gates = _topk(logits_t, b_router[l])
        row_tok, pos, blk_expert, n_valid = _routing_tables(top_idx, E, moe_rows)
        act = _moe_up(h2p, row_tok, w_gu[l], b_gu[l], blk_expert, n_valid, moe_rows)
        y = _moe_down(act, w_down[l], b_down[l], blk_expert, n_valid, moe_rows)
        x2 = _moe_combine(y, pos, gates, x1, g2, ffn_post_norm[l], S)
    return x2.reshape(B, S, D)
```

```python
import functools

import jax
import jax.numpy as jnp
from jax import lax
from jax.experimental import pallas as pl
from jax.experimental.pallas import tpu as pltpu

HEAD_DIM = 128
HG_CHUNK = 64
HG_SUB = 16
SB_BLOCK = 128
TOP_K = 4
MOE_ROWS = 768
SWIGLU_ALPHA = 1.702
SWIGLU_LIMIT = 7.0
NORM_EPS = 1e-6
VMEM_LIMIT_BYTES = 56 * 1024 * 1024

F32 = jnp.float32
BF16 = jnp.bfloat16
U32 = jnp.uint32


def _pick(dim, prefs):
    for p in prefs:
        if dim % p == 0:
            return p
    return dim


def _params(*sem):
    return pltpu.CompilerParams(dimension_semantics=sem, vmem_limit_bytes=VMEM_LIMIT_BYTES)


def _rms(t, w):
    return t * lax.rsqrt(jnp.mean(t * t, axis=-1, keepdims=True) + NORM_EPS) * w


def _dot(a, b):
    return jnp.dot(a, b, preferred_element_type=F32)


def _dot_nt(a, b):
    return lax.dot_general(a, b, (((1,), (1,)), ((), ())), preferred_element_type=F32)


def _pack_bf16_pair(lo, hi):
    lo_bits = lax.bitcast_convert_type(lo.astype(BF16).astype(F32), U32) >> 16
    hi_bits = lax.bitcast_convert_type(hi.astype(BF16).astype(F32), U32) & jnp.uint32(0xFFFF0000)
    return hi_bits | lo_bits


def _unpack_bf16_pair(words):
    lo = lax.bitcast_convert_type(words << 16, F32).astype(BF16)
    hi = lax.bitcast_convert_type(words & jnp.uint32(0xFFFF0000), F32).astype(BF16)
    return lo, hi


def _mod_kernel(c_ref, w_ref, b_ref, o_ref):
    c = c_ref[...]
    a = (c * jax.nn.sigmoid(c)).astype(BF16)
    o_ref[...] = _dot(a, w_ref[...].astype(BF16)) + b_ref[...]


def _modulation(c, w_mod, b_mod):
    B, D = c.shape
    N = w_mod.shape[1]
    tn = _pick(N, (512, 256, 128))
    return pl.pallas_call(
        _mod_kernel,
        grid=(N // tn,),
        in_specs=[pl.BlockSpec((B, D), lambda j: (0, 0)),
                  pl.BlockSpec((D, tn), lambda j: (0, j)),
                  pl.BlockSpec((1, tn), lambda j: (0, j))],
        out_specs=pl.BlockSpec((B, tn), lambda j: (0, j)),
        out_shape=jax.ShapeDtypeStruct((B, N), F32),
        compiler_params=_params("arbitrary"),
        name="modulation",
    )(c, w_mod, b_mod.reshape(1, N))


def _inproj_kernel(x_ref, nw_ref, sc_ref, sh_ref, w_ref, o_ref, h_ref):
    @pl.when(pl.program_id(1) == 0)
    def _():
        h = _rms(x_ref[...], nw_ref[...]) * (1.0 + sc_ref[...]) + sh_ref[...]
        h_ref[...] = h.astype(BF16)

    o_ref[...] = _dot(h_ref[...], w_ref[...])


def _in_proj(x2, norm_w, sc, sh, w_bf, seq):
    T, D = x2.shape
    N = w_bf.shape[1]
    tm = _pick(seq, (512, 256, 128))
    tn = _pick(N, (1024, 512, 256, 128))
    return pl.pallas_call(
        _inproj_kernel,
        grid=(T // tm, N // tn),
        in_specs=[pl.BlockSpec((tm, D), lambda i, j: (i, 0)),
                  pl.BlockSpec((1, D), lambda i, j: (0, 0)),
                  pl.BlockSpec((None, 1, D), lambda i, j: ((i * tm) // seq, 0, 0)),
                  pl.BlockSpec((None, 1, D), lambda i, j: ((i * tm) // seq, 0, 0)),
                  pl.BlockSpec((D, tn), lambda i, j: (0, j))],
        out_specs=pl.BlockSpec((tm, tn), lambda i, j: (i, j)),
        out_shape=jax.ShapeDtypeStruct((T, N), F32),
        scratch_shapes=[pltpu.VMEM((tm, D), BF16)],
        compiler_params=_params("arbitrary", "arbitrary"),
        name="in_proj",
    )(x2, norm_w.reshape(1, D), sc, sh, w_bf)


def _split3(t):
    hi = t.astype(BF16)
    r1 = t - hi.astype(F32)
    mid = r1.astype(BF16)
    lo = (r1 - mid.astype(F32)).astype(BF16)
    return hi, mid, lo


def _hgrn_kernel(lb_ref, nw_ref, q_ref, f_ref, v_ref, g_ref, o_ref, st_ref, b_s, kk_s, v_s, od_s, *, n_chunks, heads):
    C, SUB = HG_CHUNK, HG_SUB

    @pl.when(pl.program_id(2) == 0)
    def _():
        st_ref[...] = jnp.zeros_like(st_ref)

    hg = lb_ref[...]
    e = jnp.exp(hg - jnp.max(hg, axis=0, keepdims=True))
    lb_all = e[0:1, :] / jnp.sum(e, axis=0, keepdims=True)
    nw = nw_ref[...]
    row = lax.broadcasted_iota(jnp.int32, (C, C), 0)
    col = lax.broadcasted_iota(jnp.int32, (C, C), 1)
    tri = jnp.where(row >= col, 1.0, 0.0).astype(BF16)
    rows = lax.broadcasted_iota(jnp.int32, (C, 1), 0)
    rows8 = lax.broadcasted_iota(jnp.int32, (8, 1), 0)

    def lanes(h):
        return slice(h * HEAD_DIM, (h + 1) * HEAD_DIM)

    def decay(r, h):
        lb = lb_all[:, lanes(h)]
        q = q_ref[pl.ds(r, C), lanes(h)]
        v = v_ref[pl.ds(r, C), lanes(h)]
        f = lb + (1.0 - lb) * jax.nn.sigmoid(f_ref[pl.ds(r, C), lanes(h)])
        kk = 1.0 - f
        hi, mid, lo = _split3(jnp.log(f))
        b = _dot(tri, hi) + _dot(tri, mid) + _dot(tri, lo)
        b_s[h] = b
        kk_s[h] = kk
        v_s[h] = v
        return q, v, kk, b

    def matmul_terms(h, q, v, kk, b):
        b_last = b_s[h, C - 1:C, :]
        st = st_ref[h]
        o = _dot_nt((q * jnp.exp(b)).astype(BF16), st.astype(BF16))
        p = jnp.zeros((C, C), F32)
        for i in range(1, C // SUB):
            bi = b_s[h, i * SUB - 1:i * SUB, :]
            in_i = (rows >= i * SUB) & (rows < (i + 1) * SUB)
            qh = jnp.where(in_i, q * jnp.exp(jnp.minimum(b - bi, 0.0)), 0.0)
            kh = jnp.where(rows < i * SUB, kk * jnp.exp(jnp.minimum(bi - b, 0.0)), 0.0)
            p = p + _dot_nt(qh.astype(BF16), kh.astype(BF16))
        kt = (kk * jnp.exp(b_last - b)).astype(BF16)
        st_ref[h] = st * jnp.exp(b_last) + _dot(v.T.astype(BF16), kt)
        return o, p

    def pairwise(h, q, b):
        for i in range(C // SUB):
            for g0 in range(0, SUB, 8):
                r0 = i * SUB + g0
                qi = q[r0:r0 + 8, :]
                bi = b[r0:r0 + 8, :]
                acc = jnp.zeros((8, HEAD_DIM), F32)
                for s in range(g0 + 8):
                    rs = i * SUB + s
                    a = qi * jnp.exp(bi - b_s[h, rs:rs + 1, :]) * kk_s[h, rs:rs + 1, :]
                    sc = jnp.sum(a, axis=-1, keepdims=True)
                    if s >= g0:
                        sc = jnp.where(rows8 >= s - g0, sc, 0.0)
                    acc = acc + sc * v_s[h, rs:rs + 1, :]
                od_s[h, r0:r0 + 8, :] = acc

    def chunk(c, carry):
        r = pl.multiple_of(c * C, C)
        hs = range(heads)
        qvkb = [decay(r, h) for h in hs]
        op = [matmul_terms(h, *qvkb[h]) for h in hs]
        for h in hs:
            pairwise(h, qvkb[h][0], qvkb[h][3])
        for h in hs:
            o, p = op[h]
            o = o + _dot(p.astype(BF16), qvkb[h][1].astype(BF16)) + od_s[h]
            g = g_ref[pl.ds(r, C), lanes(h)]
            o_ref[pl.ds(r, C), lanes(h)] = (_rms(o, nw) * (g * jax.nn.sigmoid(g))).astype(BF16)
        return carry

    lax.fori_loop(0, n_chunks, chunk, 0)


def _hgrn2(proj, hg_lb, hg_norm_w, batch, seq, d_hg):
    T = proj.shape[0]
    H = d_hg // HEAD_DIM
    G = _pick(H, (4, 2, 1))
    W = G * HEAD_DIM
    L = _pick(seq, (512, 256, 128, 64))
    nl = seq // L
    n_slots = hg_lb.shape[0]

    def col(off):
        return pl.BlockSpec((L, W), lambda b, h, s: (b * nl + s, off * (H // G) + h))

    return pl.pallas_call(
        functools.partial(_hgrn_kernel, n_chunks=L // HG_CHUNK, heads=G),
        grid=(batch, H // G, nl),
        in_specs=[pl.BlockSpec((n_slots, W), lambda b, h, s: (0, h)),
                  pl.BlockSpec((1, HEAD_DIM), lambda b, h, s: (0, 0)),
                  col(0), col(1), col(2), col(3)],
        out_specs=pl.BlockSpec((L, W), lambda b, h, s: (b * nl + s, h)),
        out_shape=jax.ShapeDtypeStruct((T, d_hg), BF16),
        scratch_shapes=[pltpu.VMEM((G, HEAD_DIM, HEAD_DIM), F32)] + [pltpu.VMEM((G, HG_CHUNK, HEAD_DIM), F32)] * 4,
        compiler_params=_params("arbitrary", "arbitrary", "arbitrary"),
        name="hgrn2",
    )(hg_lb, hg_norm_w.reshape(1, HEAD_DIM), proj, proj, proj, proj)


def _sb_kernel(q_ref, k_ref, v_ref, o_ref, acc_ref, tail_ref, *, heads, tq):
    TK = SB_BLOCK
    i = pl.program_id(2)
    n_kb = (i + 1) * (tq // TK)
    scale = HEAD_DIM ** -0.5
    row = lax.broadcasted_iota(jnp.int32, (tq, TK), 0)
    col = lax.broadcasted_iota(jnp.int32, (tq, TK), 1)
    r2 = lax.broadcasted_iota(jnp.int32, (2 * TK, 2 * TK), 0)
    c2 = lax.broadcasted_iota(jnp.int32, (2 * TK, 2 * TK), 1)
    r2 = jnp.where(r2 >= TK, r2 - TK, r2)
    op = jnp.where((c2 >= TK) | (r2 >= c2), 1.0, 0.0).astype(BF16)
    acc_ref[...] = jnp.zeros_like(acc_ref)
    tail_ref[...] = jnp.zeros_like(tail_ref)

    def body(jj, carry, *, masked):
        j = n_kb - 1 - jj
        r = pl.multiple_of(j * TK, TK)
        strict = (col + j * TK) < (row + i * tq)
        hs = range(heads)
        ls = [slice(h * HEAD_DIM, (h + 1) * HEAD_DIM) for h in hs]
        zs = [_dot_nt(q_ref[:, ls[h]].astype(BF16), k_ref[pl.ds(r, TK), ls[h]].astype(BF16)) * scale for h in hs]
        sums = []
        for z in zs:
            lk = -(jnp.maximum(z, 0.0) + jnp.log(1.0 + jnp.exp(-jnp.abs(z))))
            if masked:
                lk = jnp.where(strict, lk, 0.0)
            hi = lk.astype(BF16)
            lo = (lk - hi.astype(F32)).astype(BF16)
            sums.append(_dot(jnp.concatenate([hi, lo], axis=1), op))
        for h in hs:
            log_w = zs[h] + sums[h][:, :TK] + tail_ref[h]
            if masked:
                log_w = jnp.where(strict, log_w, -jnp.inf)
            acc_ref[h] += _dot(jnp.exp(log_w).astype(BF16), v_ref[pl.ds(r, TK), ls[h]].astype(BF16))
            tail_ref[h] += sums[h][:, TK:]
        return carry

    n_diag = tq // TK
    lax.fori_loop(0, n_diag, functools.partial(body, masked=True), 0)
    lax.fori_loop(n_diag, n_kb, functools.partial(body, masked=False), 0)
    for h in range(heads):
        o_ref[:, h * HEAD_DIM:(h + 1) * HEAD_DIM] = acc_ref[h].astype(BF16)


def _stick_breaking(proj, batch, seq, d_hg, d_sb):
    T = proj.shape[0]
    H = d_sb // HEAD_DIM
    G = _pick(H, (4, 2, 1))
    W = G * HEAD_DIM
    assert (4 * d_hg) % W == 0
    tq = _pick(seq, (256, 128))
    nq = seq // tq
    base = 4 * d_hg // W
    return pl.pallas_call(
        functools.partial(_sb_kernel, heads=G, tq=tq),
        grid=(batch, H // G, nq),
        in_specs=[pl.BlockSpec((tq, W), lambda b, h, i: (b * nq + i, base + h)),
                  pl.BlockSpec((seq, W), lambda b, h, i: (b, base + H // G + h)),
                  pl.BlockSpec((seq, W), lambda b, h, i: (b, base + 2 * (H // G) + h))],
        out_specs=pl.BlockSpec((tq, W), lambda b, h, i: (b * nq + i, h)),
        out_shape=jax.ShapeDtypeStruct((T, d_sb), BF16),
        scratch_shapes=[pltpu.VMEM((G, tq, HEAD_DIM), F32), pltpu.VMEM((G, tq, HEAD_DIM), F32)],
        compiler_params=_params("arbitrary", "arbitrary", "arbitrary"),
        name="stickbreak",
    )(proj, proj, proj)


def _outproj_kernel(ohg_ref, osb_ref, w_ref, x_ref, g1_ref, pw_ref, fw_ref, sc_ref, sh_ref, wr_ref,
                    x1_ref, h2p_ref, lg_ref, *, n_hg):
    k = pl.program_id(1)

    @pl.when(k == 0)
    def _():
        x1_ref[...] = jnp.zeros_like(x1_ref)

    @pl.when(k < n_hg)
    def _():
        x1_ref[...] += _dot(ohg_ref[...], w_ref[...])

    @pl.when(k >= n_hg)
    def _():
        x1_ref[...] += _dot(osb_ref[...], w_ref[...])

    @pl.when(k == pl.num_programs(1) - 1)
    def _():
        x1 = x_ref[...] + g1_ref[...] * _rms(x1_ref[...], pw_ref[...])
        x1_ref[...] = x1
        h2 = _rms(x1, fw_ref[...]) * (1.0 + sc_ref[...]) + sh_ref[...]
        half = h2.shape[1] // 2
        h2p_ref[...] = _pack_bf16_pair(h2[:, :half], h2[:, half:])
        lg_ref[...] = _dot_nt(wr_ref[...], h2.astype(BF16))


def _out_proj(o_hg, o_sb, w_out_bf, x2, g1, post_w, ffn_w, sc2, sh2, wr_t, seq):
    T, D = x2.shape
    d_hg, d_sb = o_hg.shape[1], o_sb.shape[1]
    E = wr_t.shape[0]
    tm = _pick(seq, (512, 256, 128))
    tk = _pick(d_hg, (512, 256, 128))
    assert d_sb % tk == 0 and (D // 2) % HEAD_DIM == 0
    n_hg, n_sb = d_hg // tk, d_sb // tk
    vec = pl.BlockSpec((1, D), lambda i, k: (0, 0))
    per_b = pl.BlockSpec((None, 1, D), lambda i, k: ((i * tm) // seq, 0, 0))
    row_blk = pl.BlockSpec((tm, D), lambda i, k: (i, 0))
    return pl.pallas_call(
        functools.partial(_outproj_kernel, n_hg=n_hg),
        grid=(T // tm, n_hg + n_sb),
        in_specs=[pl.BlockSpec((tm, tk), lambda i, k: (i, jnp.minimum(k, n_hg - 1))),
                  pl.BlockSpec((tm, tk), lambda i, k: (i, jnp.maximum(k - n_hg, 0))),
                  pl.BlockSpec((tk, D), lambda i, k: (k, 0)),
                  row_blk, per_b, vec, vec, per_b, per_b,
                  pl.BlockSpec((E, D), lambda i, k: (0, 0))],
        out_specs=[row_blk, pl.BlockSpec((tm, D // 2), lambda i, k: (i, 0)),
                   pl.BlockSpec((E, tm), lambda i, k: (0, i))],
        out_shape=[jax.ShapeDtypeStruct((T, D), F32), jax.ShapeDtypeStruct((T, D // 2), U32),
                   jax.ShapeDtypeStruct((E, T), F32)],
        compiler_params=_params("arbitrary", "arbitrary"),
        name="out_proj",
    )(o_hg, o_sb, w_out_bf, x2, g1, post_w.reshape(1, D), ffn_w.reshape(1, D), sc2, sh2, wr_t)


def _topk_kernel(lg_ref, b_ref, idx_ref, gate_ref):
    l = lg_ref[...] + b_ref[...]
    E = l.shape[0]
    eid = lax.broadcasted_iota(jnp.int32, l.shape, 0)
    tops, ids = [], []
    for _ in range(TOP_K):
        m = jnp.max(l, axis=0, keepdims=True)
        sel = jnp.min(jnp.where(l == m, eid, E), axis=0, keepdims=True)
        tops.append(m)
        ids.append(sel)
        l = jnp.where(eid == sel, -jnp.inf, l)
    ex = [jnp.exp(t - tops[0]) for t in tops]
    den = ex[0]
    for t in ex[1:]:
        den = den + t
    idx_ref[...] = jnp.concatenate(ids, axis=0)
    gate_ref[...] = jnp.concatenate([t / den for t in ex], axis=0)


def _topk(logits_t, b_router):
    E, T = logits_t.shape
    tt = _pick(T, (2048, 1024, 512, 256, 128))
    return pl.pallas_call(
        _topk_kernel,
        grid=(T // tt,),
        in_specs=[pl.BlockSpec((E, tt), lambda i: (0, i)), pl.BlockSpec((E, 1), lambda i: (0, 0))],
        out_specs=[pl.BlockSpec((TOP_K, tt), lambda i: (0, i)), pl.BlockSpec((TOP_K, tt), lambda i: (0, i))],
        out_shape=[jax.ShapeDtypeStruct((TOP_K, T), jnp.int32), jax.ShapeDtypeStruct((TOP_K, T), F32)],
        compiler_params=_params("arbitrary"),
        name="topk",
    )(logits_t, b_router.reshape(E, 1))


def _moe_up_kernel(be_ref, nv_ref, tok_ref, tokn_ref, h_ref, wg_ref, wl_ref, bg_ref, bl_ref, o_ref,
                   stage, xb_ref, wb_ref, sem, *, rows, per):
    i = pl.program_id(0)
    c = pl.program_id(1)
    nv = nv_ref[0]
    valid = i < nv
    half = stage.shape[1]

    def row_copy(tok, r):
        return pltpu.make_async_copy(h_ref.at[pl.ds(tok, 1), :], stage.at[pl.ds(r, 1), :], sem)

    @pl.when((i == 0) & (c == 0))
    def _():
        def start(r, carry):
            row_copy(tok_ref[0, r], r).start()
            return carry
        lax.fori_loop(0, rows, start, 0)

    @pl.when((i <= nv) & (c == 0))
    def _():
        def wait(r, carry):
            row_copy(0, r).wait()
            return carry
        lax.fori_loop(0, rows, wait, 0, unroll=8)

    @pl.when(valid & (c == 0))
    def _():
        lo, hi = _unpack_bf16_pair(stage[...])
        xb_ref[:, :half] = lo
        xb_ref[:, half:] = hi

    @pl.when(valid)
    def _():
        tf = wg_ref.shape[1]
        wb_ref[:, :tf] = wg_ref[...].astype(BF16)
        wb_ref[:, tf:] = wl_ref[...].astype(BF16)
        gu = _dot(xb_ref[...], wb_ref[...])
        glu = gu[:, :tf] + bg_ref[...]
        lin = gu[:, tf:] + bl_ref[...]
        glu = jnp.minimum(glu, SWIGLU_LIMIT)
        lin = jnp.clip(lin, -SWIGLU_LIMIT, SWIGLU_LIMIT)
        o_ref[...] = (glu * jax.nn.sigmoid(SWIGLU_ALPHA * glu) * (lin + 1.0)).astype(BF16)
        for u in range(per):
            r = c * per + u
            row_copy(tokn_ref[0, r], r).start()

    @pl.when(jnp.logical_not(valid))
    def _():
        o_ref[...] = jnp.zeros_like(o_ref)


def _moe_up(h2p, row_tok, w_gu, b_gu, blk_expert, n_valid, rows):
    n_rows = row_tok.shape[0]
    half = h2p.shape[1]
    D = 2 * half
    E, _, two_ff = w_gu.shape
    d_ff = two_ff // 2
    tf = _pick(d_ff, (256, 128))
    nc = d_ff // tf
    nb = n_rows // rows
    assert rows % nc == 0
    tok3 = row_tok.reshape(nb, 1, rows)

    def chunk(i, c, nv):
        return jnp.where(i < nv[0], c, nc - 1)

    grid_spec = pltpu.PrefetchScalarGridSpec(
        num_scalar_prefetch=2,
        grid=(nb, nc),
        in_specs=[pl.BlockSpec((None, 1, rows), lambda i, c, be, nv: (i, 0, 0), memory_space=pltpu.SMEM),
                  pl.BlockSpec((None, 1, rows), lambda i, c, be, nv: (jnp.minimum(i + 1, nb - 1), 0, 0),
                               memory_space=pltpu.SMEM),
                  pl.BlockSpec(memory_space=pl.ANY),
                  pl.BlockSpec((None, D, tf), lambda i, c, be, nv: (be[i], 0, chunk(i, c, nv))),
                  pl.BlockSpec((None, D, tf), lambda i, c, be, nv: (be[i], 0, nc + chunk(i, c, nv))),
                  pl.BlockSpec((None, 1, tf), lambda i, c, be, nv: (be[i], 0, chunk(i, c, nv))),
                  pl.BlockSpec((None, 1, tf), lambda i, c, be, nv: (be[i], 0, nc + chunk(i, c, nv)))],
        out_specs=pl.BlockSpec((rows, tf), lambda i, c, be, nv: (i, c)),
        scratch_shapes=[pltpu.VMEM((rows, half), U32), pltpu.VMEM((rows, D), BF16), pltpu.VMEM((D, 2 * tf), BF16),
                        pltpu.SemaphoreType.DMA(())],
    )
    return pl.pallas_call(
        functools.partial(_moe_up_kernel, rows=rows, per=rows // nc),
        grid_spec=grid_spec,
        out_shape=jax.ShapeDtypeStruct((n_rows, d_ff), BF16),
        compiler_params=_params("arbitrary", "arbitrary"),
        name="moe_up",
    )(blk_expert, n_valid, tok3, tok3, h2p, w_gu, w_gu, b_gu.reshape(E, 1, two_ff), b_gu.reshape(E, 1, two_ff))


def _moe_down_kernel(be_ref, nv_ref, a_ref, w_ref, b_ref, o_ref):
    valid = pl.program_id(0) < nv_ref[0]

    @pl.when(valid)
    def _():
        y = _dot(a_ref[...], w_ref[...].astype(BF16)) + b_ref[...]
        hw = y.shape[1] // 2
        o_ref[...] = _pack_bf16_pair(y[:, :hw], y[:, hw:])

    @pl.when(jnp.logical_not(valid))
    def _():
        o_ref[...] = jnp.zeros_like(o_ref)


def _moe_down_chunk(D):
    return _pick(D, (2048, 1024, 512, 256))


def _moe_down(act, w_down, b_down, blk_expert, n_valid, rows):
    n_rows, d_ff = act.shape
    E, _, D = w_down.shape
    tn = _moe_down_chunk(D)
    nj = D // tn
    nb = n_rows // rows

    def col(i, j, nv):
        return jnp.where(i < nv[0], j, nj - 1)

    grid_spec = pltpu.PrefetchScalarGridSpec(
        num_scalar_prefetch=2,
        grid=(nb, nj),
        in_specs=[pl.BlockSpec((rows, d_ff), lambda i, j, be, nv: (jnp.minimum(i, nv[0] - 1), 0)),
                  pl.BlockSpec((None, d_ff, tn), lambda i, j, be, nv: (be[i], 0, col(i, j, nv))),
                  pl.BlockSpec((None, 1, tn), lambda i, j, be, nv: (be[i], 0, col(i, j, nv)))],
        out_specs=pl.BlockSpec((rows, tn // 2), lambda i, j, be, nv: (i, j)),
    )
    return pl.pallas_call(
        _moe_down_kernel,
        grid_spec=grid_spec,
        out_shape=jax.ShapeDtypeStruct((n_rows, D // 2), U32),
        compiler_params=_params("arbitrary", "arbitrary"),
        name="moe_down",
    )(blk_expert, n_valid, act, w_down, b_down.reshape(E, 1, D))


def _combine_kernel(pos_ref, posn_ref, gate_ref, y_ref, x1_ref, g2_ref, nw_ref, o_ref, buf, sem, *, tm, pair_w):
    i = pl.program_id(0)
    slot = i % 2

    def row_copy(src_row, s, k, r):
        return pltpu.make_async_copy(y_ref.at[pl.ds(src_row, 1), :], buf.at[s, k, pl.ds(r, 1), :], sem.at[s])

    def wait_block(s):
        for k in range(TOP_K):
            def wait(r, carry, k=k):
                row_copy(0, s, k, r).wait()
                return carry
            lax.fori_loop(0, tm, wait, 0, unroll=8)

    @pl.when(i == 0)
    def _():
        for k in range(TOP_K):
            def start(r, carry, k=k):
                row_copy(pos_ref[k, r], 0, k, r).start()
                return carry
            lax.fori_loop(0, tm, start, 0)

    wait_block(slot)

    grp = 16
    for g0 in range(0, tm, grp):
        for k in range(TOP_K):
            for r in range(g0, g0 + grp):
                row_copy(posn_ref[k, r], 1 - slot, k, r).start()
        rs = slice(g0, g0 + grp)
        m = None
        for k in range(TOP_K):
            words = buf[slot, k, rs, :]
            pieces = []
            for c0 in range(0, words.shape[1], pair_w):
                lo, hi = _unpack_bf16_pair(words[:, c0:c0 + pair_w])
                pieces += [lo.astype(F32), hi.astype(F32)]
            yk = gate_ref[k, rs, :] * jnp.concatenate(pieces, axis=1)
            m = yk if m is None else m + yk
        o_ref[rs, :] = x1_ref[rs, :] + g2_ref[...] * _rms(m, nw_ref[...])

    @pl.when(i == pl.num_programs(0) - 1)
    def _():
        wait_block(1 - slot)


def _moe_combine(y, pos, gates, x1, g2, norm_w, seq):
    T, D = x1.shape
    tm = _pick(seq, (128,))
    nt = T // tm
    pos_b = pos.reshape(TOP_K, nt, tm).transpose(1, 0, 2)
    return pl.pallas_call(
        functools.partial(_combine_kernel, tm=tm, pair_w=_moe_down_chunk(D) // 2),
        grid=(nt,),
        in_specs=[pl.BlockSpec((None, TOP_K, tm), lambda i: (i, 0, 0), memory_space=pltpu.SMEM),
                  pl.BlockSpec((None, TOP_K, tm), lambda i: (jnp.minimum(i + 1, nt - 1), 0, 0),
                               memory_space=pltpu.SMEM),
                  pl.BlockSpec((TOP_K, tm, 1), lambda i: (0, i, 0)),
                  pl.BlockSpec(memory_space=pl.ANY),
                  pl.BlockSpec((tm, D), lambda i: (i, 0)),
                  pl.BlockSpec((None, 1, D), lambda i: ((i * tm) // seq, 0, 0)),
                  pl.BlockSpec((1, D), lambda i: (0, 0))],
        out_specs=pl.BlockSpec((tm, D), lambda i: (i, 0)),
        out_shape=jax.ShapeDtypeStruct((T, D), F32),
        scratch_shapes=[pltpu.VMEM((2, TOP_K, tm, D // 2), U32), pltpu.SemaphoreType.DMA((2,))],
        compiler_params=_params("arbitrary"),
        name="moe_combine",
    )(pos_b, pos_b, gates.reshape(TOP_K, T, 1), y, x1, g2, norm_w.reshape(1, D))


def _routing_tables(top_idx, n_experts, rows):
    T = top_idx.shape[1]
    n_assign = T * TOP_K
    i32 = jnp.int32
    flat_e = top_idx.T.reshape(-1)
    iota = jnp.arange(n_assign, dtype=i32)
    _, order = lax.sort((flat_e, iota), num_keys=1)
    _, inv = lax.sort((order, iota), num_keys=1)
    eids = jnp.arange(n_experts, dtype=i32)
    counts = jnp.sum((flat_e[None, :] == eids[:, None]).astype(i32), axis=1)
    padded = (counts + rows - 1) // rows * rows
    pad_end = jnp.cumsum(padded)
    pad_start = pad_end - padded
    start = jnp.cumsum(counts) - counts
    n_rows = -(-n_assign // rows) * rows + n_experts * rows
    n_blocks = n_rows // rows
    pos = (pad_start[flat_e] + inv - start[flat_e]).astype(i32).reshape(T, TOP_K).T

    def expert_of(r):
        return jnp.minimum(jnp.sum((pad_end[None, :] <= r[:, None]).astype(i32), axis=1), n_experts - 1)

    r = jnp.arange(n_rows, dtype=i32)
    row_e = expert_of(r)
    off = r - pad_start[row_e]
    real = (off < counts[row_e]) & (r < pad_end[-1])
    src = jnp.clip(start[row_e] + off, 0, n_assign - 1)
    row_tok = jnp.where(real, order[src] // TOP_K, 0).astype(i32)
    n_valid = (pad_end[-1] // rows).astype(i32)
    blk = jnp.minimum(jnp.arange(n_blocks, dtype=i32), n_valid - 1) * rows
    return row_tok, pos, expert_of(blk).astype(i32), n_valid.reshape(1)


def kernel(x, c, w_mod, b_mod, mix_pre_norm, mix_post_norm, w_in, hg_lb, hg_norm_w, w_out, ffn_pre_norm,
           ffn_post_norm, w_router, b_router, w_gu, b_gu, w_down, b_down):
    B, S, D = x.shape
    depth = w_mod.shape[0]
    d_hg = hg_lb.shape[1]
    d_sb = w_out.shape[1] - d_hg
    E = w_router.shape[2]
    T = B * S
    assert depth == 1 and S % SB_BLOCK == 0 and d_hg % HEAD_DIM == 0 and d_sb % HEAD_DIM == 0
    moe_rows = MOE_ROWS

    x2 = x.reshape(T, D)
    for l in range(depth):
        mod = _modulation(c, w_mod[l], b_mod[l])
        sh1, sc1, g1, sh2, sc2, g2 = [mod[:, j * D:(j + 1) * D].reshape(B, 1, D) for j in range(6)]

        proj = _in_proj(x2, mix_pre_norm[l], sc1, sh1, w_in[l].astype(BF16), S)
        o_hg = _hgrn2(proj, hg_lb, hg_norm_w[l], B, S, d_hg)
        o_sb = _stick_breaking(proj, B, S, d_hg, d_sb)
        x1, h2p, logits_t = _out_proj(o_hg, o_sb, w_out[l].astype(BF16), x2, g1, mix_post_norm[l],
                                      ffn_pre_norm[l], sc2, sh2, w_router[l].T.astype(BF16), S)

        top_idx, gates = _topk(logits_t, b_router[l])
        row_tok, pos, blk_expert, n_valid = _routing_tables(top_idx, E, moe_rows)
        act = _moe_up(h2p, row_tok, w_gu[l], b_gu[l], blk_expert, n_valid, moe_rows)
        y = _moe_down(act, w_down[l], b_down[l], blk_expert, n_valid, moe_rows)
        x2 = _moe_combine(y, pos, gates, x1, g2, ffn_post_norm[l], S)
    return x2.reshape(B, S, D)
```
